```python
import jax, jax.numpy as jnp
from jax import lax
import numpy as np

D_MODEL = 2048
BATCH = 4
SEQ = 4096
DEPTH = 2

N_A_LAYERS = DEPTH // 2
N_B_LAYERS = DEPTH - N_A_LAYERS
POOL_WINDOWS = (2, 4, 8, 16)
N_POOL_GROUPS = len(POOL_WINDOWS)
POOL_GROUP_DIM = D_MODEL // N_POOL_GROUPS
HEAD_DIM = 64
N_Q_HEADS = D_MODEL // HEAD_DIM
N_KV_HEADS = N_Q_HEADS // 8
GQA_GROUP = N_Q_HEADS // N_KV_HEADS
ATTN_WIDTH = N_Q_HEADS * HEAD_DIM
KV_WIDTH = N_KV_HEADS * HEAD_DIM
WINDOW = 128
BLOCK = 128
ROPE_THETA = 10000.0
LN_EPS = 1e-5
NEG_INF = -1e30
DEEPNORM_ALPHA = (2 * DEPTH) ** 0.25
DEEPNORM_BETA = (8 * DEPTH) ** -0.25

kernel_name = "yoco_pool_swa_sink_hybrid"


def layer_norm(x, g, b):
    xf = x.astype(jnp.float32)
    mu = jnp.mean(xf, axis=-1, keepdims=True)
    var = jnp.mean(jnp.square(xf - mu), axis=-1, keepdims=True)
    y = (xf - mu) * lax.rsqrt(var + LN_EPS) * g.astype(jnp.float32) + b.astype(jnp.float32)
    return y.astype(x.dtype)


def rope(t, pos):
    d = t.shape[-1]
    inv_freq = ROPE_THETA ** (-jnp.arange(0, d, 2, dtype=jnp.float32) / d)
    ang = pos[:, None] * inv_freq[None, :]
    ang = jnp.concatenate([ang, ang], axis=-1)[:, None, :]
    tf = t.astype(jnp.float32)
    t1, t2 = tf[..., : d // 2], tf[..., d // 2:]
    rot = jnp.concatenate([-t2, t1], axis=-1)
    return (tf * jnp.cos(ang) + rot * jnp.sin(ang)).astype(t.dtype)


def causal_window_mean(u, w):
    s = u.shape[1]
    c = jnp.cumsum(u, axis=1)
    c_prev = jnp.pad(c, ((0, 0), (w, 0), (0, 0)))[:, :s]
    count = jnp.minimum(jnp.arange(s) + 1, w).astype(jnp.float32)
    return (c - c_prev) / count[None, :, None]


def pool_mixer(x, w_in, w_group, scale, w_out):
    b, s, _ = x.shape
    h = x @ w_in
    u, z = h[..., :D_MODEL], h[..., D_MODEL:]
    ug = u.astype(jnp.float32).reshape(b, s, N_POOL_GROUPS, POOL_GROUP_DIM)
    pooled = jnp.stack(
        [causal_window_mean(ug[:, :, g], w) - ug[:, :, g] for g, w in enumerate(POOL_WINDOWS)],
        axis=2)
    mixed = jnp.einsum('bsgc,gcd->bsgd', pooled.astype(x.dtype), w_group).reshape(b, s, D_MODEL)
    y = mixed * scale * jax.nn.silu(z)
    return y @ w_out


def shared_kv(x, w_k, w_v, pos):
    b, s, _ = x.shape
    k = rope((x @ w_k).reshape(b, s, N_KV_HEADS, HEAD_DIM), pos)
    v = (x @ w_v).reshape(b, s, N_KV_HEADS, HEAD_DIM)
    return k, v


def band(t, nb):
    b = t.shape[0]
    tp = jnp.pad(t, ((0, 0), (BLOCK, 0), (0, 0), (0, 0)))
    tb = tp.reshape(b, nb + 1, BLOCK, t.shape[2], t.shape[3])
    return jnp.concatenate([tb[:, :-1], tb[:, 1:]], axis=2)


def banded_swa_with_sinks(q, k, v, sinks):
    b, s, _, d = q.shape
    nb = s // BLOCK
    qb = q.reshape(b, nb, BLOCK, N_KV_HEADS, GQA_GROUP, d)
    kb, vb = band(k, nb), band(v, nb)
    scores = jnp.einsum('bnqkgd,bnskd->bnkgqs', qb, kb,
                        preferred_element_type=jnp.float32) * (d ** -0.5)
    blk = jnp.arange(nb)[:, None, None]
    q_pos = blk * BLOCK + jnp.arange(BLOCK)[None, :, None]
    k_pos = (blk - 1) * BLOCK + jnp.arange(2 * BLOCK)[None, None, :]
    valid = (k_pos <= q_pos) & (k_pos > q_pos - WINDOW) & (k_pos >= 0)
    scores = jnp.where(valid[None, :, None, None], scores, NEG_INF)
    sink = sinks.astype(jnp.float32).reshape(N_KV_HEADS, GQA_GROUP)[None, None, :, :, None, None]
    m = jnp.maximum(jnp.max(scores, axis=-1, keepdims=True), sink)
    p = jnp.exp(scores - m)
    probs = p / (jnp.sum(p, axis=-1, keepdims=True) + jnp.exp(sink - m))
    out = jnp.einsum('bnkgqs,bnskd->bnqkgd', probs.astype(v.dtype), vb)
    return out.reshape(b, s, N_Q_HEADS * d)


def swa_mixer(x, k, v, w_qg, sinks, w_out, pos):
    b, s, _ = x.shape
    h = x @ w_qg
    q = rope(h[..., :ATTN_WIDTH].reshape(b, s, N_Q_HEADS, HEAD_DIM), pos)
    z = h[..., ATTN_WIDTH:]
    o = banded_swa_with_sinks(q, k, v, sinks)
    return (o * jax.nn.silu(z)) @ w_out


def setup_inputs(seed: int = 0) -> dict:
    key = jax.random.key(seed)
    ks = jax.random.split(key, 16)
    f32 = jnp.float32
    nrm = lambda k, shape, fan_in: jax.random.normal(k, shape, f32) * fan_in ** -0.5
    return {
        "x": jax.random.normal(ks[0], (BATCH, SEQ, D_MODEL), f32),
        "ln_g": 1.0 + 0.02 * jax.random.normal(ks[1], (DEPTH, D_MODEL), f32),
        "ln_b": 0.02 * jax.random.normal(ks[2], (DEPTH, D_MODEL), f32),
        "a_w_in": nrm(ks[3], (N_A_LAYERS, D_MODEL, 2 * D_MODEL), D_MODEL),
        "a_w_group": nrm(ks[4], (N_A_LAYERS, N_POOL_GROUPS, POOL_GROUP_DIM, POOL_GROUP_DIM), POOL_GROUP_DIM),
        "a_scale": 1.0 + 0.02 * jax.random.normal(ks[5], (N_A_LAYERS, D_MODEL), f32),
        "a_w_out": nrm(ks[6], (N_A_LAYERS, D_MODEL, D_MODEL), D_MODEL) * DEEPNORM_BETA,
        "b_w_k": nrm(ks[7], (D_MODEL, KV_WIDTH), D_MODEL),
        "b_w_v": nrm(ks[8], (D_MODEL, KV_WIDTH), D_MODEL) * DEEPNORM_BETA,
        "b_w_qg": nrm(ks[9], (N_B_LAYERS, D_MODEL, 2 * ATTN_WIDTH), D_MODEL),
        "b_sinks": 0.5 * jax.random.normal(ks[10], (N_B_LAYERS, N_Q_HEADS), f32),
        "b_w_out": nrm(ks[11], (N_B_LAYERS, ATTN_WIDTH, D_MODEL), ATTN_WIDTH) * DEEPNORM_BETA,
    }


def reference(x, ln_g, ln_b, a_w_in, a_w_group, a_scale, a_w_out,
              b_w_k, b_w_v, b_w_qg, b_sinks, b_w_out):
    s = x.shape[1]
    pos = jnp.arange(s, dtype=jnp.float32)
    k_sh, v_sh = None, None
    for i in range(DEPTH):
        if i < N_A_LAYERS:
            y = pool_mixer(x, a_w_in[i], a_w_group[i], a_scale[i], a_w_out[i])
        else:
            j = i - N_A_LAYERS
            if j == 0:
                k_sh, v_sh = shared_kv(x, b_w_k, b_w_v, pos)
            y = swa_mixer(x, k_sh, v_sh, b_w_qg[j], b_sinks[j], b_w_out[j], pos)
        x = layer_norm(DEEPNORM_ALPHA * x + y, ln_g[i], ln_b[i])
    return x
```

```python
import functools

import jax
import jax.numpy as jnp
from jax import lax
from jax.experimental import pallas as pl
from jax.experimental.pallas import tpu as pltpu

D_MODEL = 2048
DEPTH = 2
POOL_WINDOWS = (2, 4, 8, 16)
N_POOL_GROUPS = len(POOL_WINDOWS)
POOL_GROUP_DIM = D_MODEL // N_POOL_GROUPS
MAX_WINDOW = max(POOL_WINDOWS)
HEAD_DIM = 64
HALF_DIM = HEAD_DIM // 2
N_Q_HEADS = D_MODEL // HEAD_DIM
N_KV_HEADS = N_Q_HEADS // 8
GQA_GROUP = N_Q_HEADS // N_KV_HEADS
KV_WIDTH = N_KV_HEADS * HEAD_DIM
ATTN_BLOCK = 128
ROPE_THETA = 10000.0
LN_EPS = 1e-5
NEG_INF = -1e30
DEEPNORM_ALPHA = (2 * DEPTH) ** 0.25

LANES = 128
HEADS_PER_VREG = LANES // HEAD_DIM
PAIRS_PER_KV = GQA_GROUP // HEADS_PER_VREG
VMEM_LIMIT_BYTES = 56 * 1024 * 1024
TILE_A = 256
TILE_B = 256

F32 = jnp.float32
BF16 = jnp.bfloat16


def _dot(a, b):
    return jnp.dot(a, b, preferred_element_type=F32)


def _silu(z):
    return z / (1.0 + jnp.exp(-z))


def _layer_norm(r, g, b):
    mu = jnp.mean(r, axis=-1, keepdims=True)
    c = r - mu
    var = jnp.mean(c * c, axis=-1, keepdims=True)
    return c * lax.rsqrt(var + LN_EPS) * g + b


def _rope(t, cos, sin_signed):
    width = t.shape[1]
    reps = width // LANES
    cos_w = jnp.concatenate([cos] * reps, axis=1)
    sin_w = jnp.concatenate([sin_signed] * reps, axis=1)
    lane = lax.broadcasted_iota(jnp.int32, (1, width), 1)
    first_half = (lane % HEAD_DIM) < HALF_DIM
    partner = jnp.where(first_half,
                        pltpu.roll(t, width - HALF_DIM, axis=1),
                        pltpu.roll(t, HALF_DIM, axis=1))
    return t * cos_w + partner * sin_w


def _layer_a_kernel(x_ref, win_ref, wg_ref, scale_ref, wout_ref, g_ref, b_ref, wkv_ref,
                    cos_ref, sin_ref, x1_ref, k_ref, v_ref, carry_ref, y_ref):
    tm = x_ref.shape[1]
    i = pl.program_id(1)

    @pl.when(i == 0)
    def _():
        carry_ref[...] = jnp.zeros_like(carry_ref)

    x = x_ref[0]
    xb = x.astype(BF16)
    pos1 = lax.broadcasted_iota(jnp.int32, (tm, 1), 0) + (i * tm + 1)
    for g, w in enumerate(POOL_WINDOWS):
        lo, hi = g * POOL_GROUP_DIM, (g + 1) * POOL_GROUP_DIM
        u = _dot(xb, win_ref[:, lo:hi])
        z = _dot(xb, win_ref[:, D_MODEL + lo:D_MODEL + hi])
        acc = jnp.concatenate([carry_ref[:, lo:hi], u], axis=0)
        carry_ref[:, lo:hi] = u[tm - MAX_WINDOW:, :]
        shift = 1
        while shift < w:
            acc = acc + pltpu.roll(acc, shift, axis=0)
            shift *= 2
        inv_count = 1.0 / jnp.minimum(pos1, w).astype(F32)
        pooled = acc[MAX_WINDOW:, :] * inv_count - u
        mixed = _dot(pooled.astype(BF16), wg_ref[g])
        y = mixed * scale_ref[:, lo:hi] * _silu(z)
        y_ref[:, lo:hi] = y.astype(BF16)

    r = DEEPNORM_ALPHA * x + _dot(y_ref[...], wout_ref[...])
    x1 = _layer_norm(r, g_ref[...], b_ref[...])
    x1_ref[0] = x1
    kv = _dot(x1.astype(BF16), wkv_ref[...])
    k = _rope(kv[:, :KV_WIDTH], cos_ref[...], sin_ref[...])
    k_ref[0] = k.astype(BF16)
    v_ref[0] = kv[:, KV_WIDTH:].astype(BF16)


def _pad_heads(t):
    lane = lax.broadcasted_iota(jnp.int32, (1, LANES), 1)
    low = lane < HEAD_DIM
    lo_cols, hi_cols = [], []
    for j in range(KV_WIDTH // LANES):
        col = t[:, j * LANES:(j + 1) * LANES]
        swapped = pltpu.roll(col, HEAD_DIM, axis=1)
        lo_cols += [jnp.where(low, col, 0.0), jnp.where(low, swapped, 0.0)]
        hi_cols += [jnp.where(low, 0.0, swapped), jnp.where(low, 0.0, col)]
    return (jnp.concatenate(lo_cols, axis=1).astype(BF16),
            jnp.concatenate(hi_cols, axis=1).astype(BF16))


def _layer_b_kernel(sinks_ref, x1_ref, kc_ref, kp_ref, vc_ref, vp_ref, wqg_ref, wout_ref,
                    g_ref, b_ref, cos_ref, sin_ref, out_ref,
                    q_ref, k0_ref, k1_ref, v0_ref, v1_ref, p_ref, o_ref):
    tq = x1_ref.shape[1]
    i = pl.program_id(1)
    x1 = x1_ref[0]
    xb = x1.astype(BF16)

    q = _rope(_dot(xb, wqg_ref[:, :D_MODEL]), cos_ref[...], sin_ref[...])
    q_ref[...] = (q * (HEAD_DIM ** -0.5)).astype(BF16)

    kf = jnp.concatenate([kp_ref[0], kc_ref[0]], axis=0).astype(F32)
    vf = jnp.concatenate([vp_ref[0], vc_ref[0]], axis=0).astype(F32)
    k0_ref[...], k1_ref[...] = _pad_heads(kf)
    v0_ref[...], v1_ref[...] = _pad_heads(vf)

    qrow = lax.broadcasted_iota(jnp.int32, (ATTN_BLOCK, 2 * ATTN_BLOCK), 0)
    kcol = lax.broadcasted_iota(jnp.int32, (ATTN_BLOCK, 2 * ATTN_BLOCK), 1)
    band = (kcol > qrow) & (kcol <= qrow + ATTN_BLOCK)
    lane = lax.broadcasted_iota(jnp.int32, (1, LANES), 1)
    low = lane < HEAD_DIM

    for n in range(tq // ATTN_BLOCK):
        rows = slice(n * ATTN_BLOCK, (n + 1) * ATTN_BLOCK)
        keys = slice(n * ATTN_BLOCK, (n + 2) * ATTN_BLOCK)
        if n == 0:
            valid = band & ((kcol >= ATTN_BLOCK) | (i > 0))
        else:
            valid = band
        for g in range(N_KV_HEADS):
            cols = slice(g * LANES, (g + 1) * LANES)
            kk = jnp.concatenate([k0_ref[keys, cols], k1_ref[keys, cols]], axis=0)
            vv = jnp.concatenate([v0_ref[keys, cols], v1_ref[keys, cols]], axis=0)
            pair0 = g * PAIRS_PER_KV
            qg = jnp.concatenate(
                [q_ref[rows, (pair0 + p) * LANES:(pair0 + p + 1) * LANES]
                 for p in range(PAIRS_PER_KV)], axis=0)
            s_all = lax.dot_general(qg, kk, (((1,), (1,)), ((), ())),
                                    preferred_element_type=F32)
            inv_l = []
            for p in range(PAIRS_PER_KV):
                for e in range(HEADS_PER_VREG):
                    head = (pair0 + p) * HEADS_PER_VREG + e
                    sink = sinks_ref[head]
                    s = s_all[p * ATTN_BLOCK:(p + 1) * ATTN_BLOCK,
                              e * 2 * ATTN_BLOCK:(e + 1) * 2 * ATTN_BLOCK]
                    s = jnp.where(valid, s, NEG_INF)
                    m = jnp.maximum(jnp.max(s, axis=-1, keepdims=True), sink)
                    pe = jnp.exp(s - m)
                    l = jnp.sum(pe, axis=-1, keepdims=True) + jnp.exp(sink - m)
                    inv_l.append(1.0 / l)
                    p_ref[p * ATTN_BLOCK:(p + 1) * ATTN_BLOCK,
                          e * 2 * ATTN_BLOCK:(e + 1) * 2 * ATTN_BLOCK] = pe.astype(BF16)
            o_all = _dot(p_ref[...], vv)
            for p in range(PAIRS_PER_KV):
                norm = jnp.where(low, inv_l[2 * p], inv_l[2 * p + 1])
                o_ref[rows, (pair0 + p) * LANES:(pair0 + p + 1) * LANES] = (
                    o_all[p * ATTN_BLOCK:(p + 1) * ATTN_BLOCK, :] * norm)

    z = _dot(xb, wqg_ref[:, D_MODEL:])
    gated = (o_ref[...] * _silu(z)).astype(BF16)
    r = DEEPNORM_ALPHA * x1 + _dot(gated, wout_ref[...])
    out_ref[0] = _layer_norm(r, g_ref[...], b_ref[...])


def _resident(shape):
    return pl.BlockSpec(shape, lambda *_: (0,) * len(shape), pipeline_mode=pl.Buffered(1))


def _rope_tables(seq):
    inv_freq = ROPE_THETA ** (-jnp.arange(0, HEAD_DIM, 2, dtype=F32) / HEAD_DIM)
    ang = jnp.arange(seq, dtype=F32)[:, None] * inv_freq[None, :]
    ang = jnp.concatenate([ang, ang], axis=-1)
    sign = jnp.concatenate([-jnp.ones((HALF_DIM,), F32), jnp.ones((HALF_DIM,), F32)])
    cos = jnp.tile(jnp.cos(ang), (1, HEADS_PER_VREG))
    sin_signed = jnp.tile(jnp.sin(ang) * sign, (1, HEADS_PER_VREG))
    return cos, sin_signed


def _layer_a(x, w_in, w_group, scale, w_out, ln_g, ln_b, w_kv, cos, sin_signed):
    batch, seq, _ = x.shape
    tm = TILE_A
    tok = lambda width: pl.BlockSpec((1, tm, width), lambda b, i: (b, i, 0))
    table = pl.BlockSpec((tm, LANES), lambda b, i: (i, 0))
    return pl.pallas_call(
        _layer_a_kernel,
        grid=(batch, seq // tm),
        in_specs=[
            tok(D_MODEL),
            _resident((D_MODEL, 2 * D_MODEL)),
            _resident((N_POOL_GROUPS, POOL_GROUP_DIM, POOL_GROUP_DIM)),
            _resident((1, D_MODEL)),
            _resident((D_MODEL, D_MODEL)),
            _resident((1, D_MODEL)),
            _resident((1, D_MODEL)),
            _resident((D_MODEL, 2 * KV_WIDTH)),
            table, table,
        ],
        out_specs=[tok(D_MODEL), tok(KV_WIDTH), tok(KV_WIDTH)],
        out_shape=[
            jax.ShapeDtypeStruct((batch, seq, D_MODEL), F32),
            jax.ShapeDtypeStruct((batch, seq, KV_WIDTH), BF16),
            jax.ShapeDtypeStruct((batch, seq, KV_WIDTH), BF16),
        ],
        scratch_shapes=[
            pltpu.VMEM((MAX_WINDOW, D_MODEL), F32),
            pltpu.VMEM((tm, D_MODEL), BF16),
        ],
        compiler_params=pltpu.CompilerParams(
            dimension_semantics=("arbitrary", "arbitrary"),
            vmem_limit_bytes=VMEM_LIMIT_BYTES),
        name="yoco_pool_layer",
    )(x, w_in, w_group, scale, w_out, ln_g, ln_b, w_kv, cos, sin_signed)


def _layer_b(x1, k, v, w_qg, sinks, w_out, ln_g, ln_b, cos, sin_signed):
    batch, seq, _ = x1.shape
    tq = TILE_B
    blocks_per_tile = tq // ATTN_BLOCK
    tok = lambda width: pl.BlockSpec((1, tq, width), lambda b, i: (b, i, 0))
    prev = pl.BlockSpec((1, ATTN_BLOCK, KV_WIDTH),
                        lambda b, i: (b, jnp.maximum(i * blocks_per_tile - 1, 0), 0))
    table = pl.BlockSpec((tq, LANES), lambda b, i: (i, 0))
    padded = N_KV_HEADS * LANES
    return pl.pallas_call(
        _layer_b_kernel,
        grid=(batch, seq // tq),
        in_specs=[
            pl.BlockSpec(memory_space=pltpu.SMEM),
            tok(D_MODEL),
            tok(KV_WIDTH), prev, tok(KV_WIDTH), prev,
            _resident((D_MODEL, 2 * D_MODEL)),
            _resident((D_MODEL, D_MODEL)),
            _resident((1, D_MODEL)),
            _resident((1, D_MODEL)),
            table, table,
        ],
        out_specs=tok(D_MODEL),
        out_shape=jax.ShapeDtypeStruct((batch, seq, D_MODEL), F32),
        scratch_shapes=[
            pltpu.VMEM((tq, D_MODEL), BF16),
            pltpu.VMEM((tq + ATTN_BLOCK, padded), BF16),
            pltpu.VMEM((tq + ATTN_BLOCK, padded), BF16),
            pltpu.VMEM((tq + ATTN_BLOCK, padded), BF16),
            pltpu.VMEM((tq + ATTN_BLOCK, padded), BF16),
            pltpu.VMEM((PAIRS_PER_KV * ATTN_BLOCK, HEADS_PER_VREG * 2 * ATTN_BLOCK), BF16),
            pltpu.VMEM((tq, D_MODEL), F32),
        ],
        compiler_params=pltpu.CompilerParams(
            dimension_semantics=("arbitrary", "arbitrary"),
            vmem_limit_bytes=VMEM_LIMIT_BYTES),
        name="yoco_swa_layer",
    )(sinks, x1, k, k, v, v, w_qg, w_out, ln_g, ln_b, cos, sin_signed)


def kernel(x, ln_g, ln_b, a_w_in, a_w_group, a_scale, a_w_out, b_w_k, b_w_v, b_w_qg, b_sinks, b_w_out):
    assert a_w_in.shape[0] == 1 and b_w_qg.shape[0] == 1
    seq = x.shape[1]
    cos, sin_signed = _rope_tables(seq)
    w_kv = jnp.concatenate([b_w_k, b_w_v], axis=1).astype(BF16)
    x1, k, v = _layer_a(
        x, a_w_in[0].astype(BF16), a_w_group[0].astype(BF16), a_scale[0][None, :],
        a_w_out[0].astype(BF16), ln_g[0][None, :], ln_b[0][None, :], w_kv, cos, sin_signed)
    return _layer_b(
        x1, k, v, b_w_qg[0].astype(BF16), b_sinks[0], b_w_out[0].astype(BF16),
        ln_g[1][None, :], ln_b[1][None, :], cos, sin_signed)
```

```python
import functools

import jax
import jax.numpy as jnp
from jax import lax
from jax.experimental import pallas as pl
from jax.experimental.pallas import tpu as pltpu

D_MODEL = 2048
DEPTH = 2
POOL_WINDOWS = (2, 4, 8, 16)
N_POOL_GROUPS = len(POOL_WINDOWS)
POOL_GROUP_DIM = D_MODEL // N_POOL_GROUPS
MAX_WINDOW = max(POOL_WINDOWS)
HEAD_DIM = 64
HALF_DIM = HEAD_DIM // 2
N_Q_HEADS = D_MODEL // HEAD_DIM
N_KV_HEADS = N_Q_HEADS // 8
GQA_GROUP = N_Q_HEADS // N_KV_HEADS
KV_WIDTH = N_KV_HEADS * HEAD_DIM
ATTN_BLOCK = 128
ROPE_THETA = 10000.0
LN_EPS = 1e-5
NEG_INF = -1e30
LOG2_E = 1.4426950408889634
DEEPNORM_ALPHA = (2 * DEPTH) ** 0.25

LANES = 128
HEADS_PER_VREG = LANES // HEAD_DIM
PAIRS_PER_KV = GQA_GROUP // HEADS_PER_VREG
VMEM_LIMIT_BYTES = 56 * 1024 * 1024
TILE_A = 256
TILE_B = 256

F32 = jnp.float32
BF16 = jnp.bfloat16


def _dot(a, b):
    return jnp.dot(a, b, preferred_element_type=F32)


def _silu(z):
    return z / (1.0 + jnp.exp(-z))


def _layer_norm(r, g, b):
    mu = jnp.mean(r, axis=-1, keepdims=True)
    c = r - mu
    var = jnp.mean(c * c, axis=-1, keepdims=True)
    return c * lax.rsqrt(var + LN_EPS) * g + b


def _rope(t, cos, sin_signed):
    width = t.shape[1]
    reps = width // LANES
    cos_w = jnp.concatenate([cos] * reps, axis=1)
    sin_w = jnp.concatenate([sin_signed] * reps, axis=1)
    lane = lax.broadcasted_iota(jnp.int32, (1, width), 1)
    first_half = (lane % HEAD_DIM) < HALF_DIM
    partner = jnp.where(first_half,
                        pltpu.roll(t, width - HALF_DIM, axis=1),
                        pltpu.roll(t, HALF_DIM, axis=1))
    return t * cos_w + partner * sin_w


def _layer_a_kernel(x_ref, win_ref, wg_ref, scale_ref, wout_ref, g_ref, b_ref, wkv_ref,
                    cos_ref, sin_ref, x1_ref, k_ref, v_ref, carry_ref, y_ref):
    tm = x_ref.shape[1]
    i = pl.program_id(1)

    @pl.when(i == 0)
    def _():
        carry_ref[...] = jnp.zeros_like(carry_ref)

    x = x_ref[0]
    xb = x.astype(BF16)
    pos1 = lax.broadcasted_iota(jnp.int32, (tm, 1), 0) + (i * tm + 1)
    for g, w in enumerate(POOL_WINDOWS):
        lo, hi = g * POOL_GROUP_DIM, (g + 1) * POOL_GROUP_DIM
        u = _dot(xb, win_ref[:, lo:hi])
        z = _dot(xb, win_ref[:, D_MODEL + lo:D_MODEL + hi])
        acc = jnp.concatenate([carry_ref[:, lo:hi], u], axis=0)
        carry_ref[:, lo:hi] = u[tm - MAX_WINDOW:, :]
        shift = 1
        while shift < w:
            acc = acc + pltpu.roll(acc, shift, axis=0)
            shift *= 2
        inv_count = 1.0 / jnp.minimum(pos1, w).astype(F32)
        pooled = acc[MAX_WINDOW:, :] * inv_count - u
        mixed = _dot(pooled.astype(BF16), wg_ref[g])
        y = mixed * scale_ref[:, lo:hi] * _silu(z)
        y_ref[:, lo:hi] = y.astype(BF16)

    r = DEEPNORM_ALPHA * x + _dot(y_ref[...], wout_ref[...])
    x1 = _layer_norm(r, g_ref[...], b_ref[...])
    x1_ref[0] = x1
    kv = _dot(x1.astype(BF16), wkv_ref[...])
    k = _rope(kv[:, :KV_WIDTH], cos_ref[...], sin_ref[...])
    k_ref[0] = k.astype(BF16)
    v_ref[0] = kv[:, KV_WIDTH:].astype(BF16)


def _pad_heads(t):
    lane = lax.broadcasted_iota(jnp.int32, (1, LANES), 1)
    low = lane < HEAD_DIM
    lo_cols, hi_cols = [], []
    for j in range(KV_WIDTH // LANES):
        col = t[:, j * LANES:(j + 1) * LANES]
        swapped = pltpu.roll(col, HEAD_DIM, axis=1)
        lo_cols += [jnp.where(low, col, 0.0), jnp.where(low, swapped, 0.0)]
        hi_cols += [jnp.where(low, 0.0, swapped), jnp.where(low, 0.0, col)]
    return (jnp.concatenate(lo_cols, axis=1).astype(BF16),
            jnp.concatenate(hi_cols, axis=1).astype(BF16))


def _layer_b_kernel(sinks_ref, x1_ref, kc_ref, kp_ref, vc_ref, vp_ref, wqg_ref, wout_ref,
                    g_ref, b_ref, cos_ref, sin_ref, out_ref,
                    q_ref, k0_ref, k1_ref, v0_ref, v1_ref, p_ref, o_ref):
    tq = x1_ref.shape[1]
    i = pl.program_id(1)
    x1 = x1_ref[0]
    xb = x1.astype(BF16)

    q = _rope(_dot(xb, wqg_ref[:, :D_MODEL]), cos_ref[...], sin_ref[...])
    q_ref[...] = (q * (HEAD_DIM ** -0.5 * LOG2_E)).astype(BF16)

    kf = jnp.concatenate([kp_ref[0], kc_ref[0]], axis=0).astype(F32)
    vf = jnp.concatenate([vp_ref[0], vc_ref[0]], axis=0).astype(F32)
    k0_ref[...], k1_ref[...] = _pad_heads(kf)
    v0_ref[...], v1_ref[...] = _pad_heads(vf)

    qrow = lax.broadcasted_iota(jnp.int32, (ATTN_BLOCK, 2 * ATTN_BLOCK), 0)
    kcol = lax.broadcasted_iota(jnp.int32, (ATTN_BLOCK, 2 * ATTN_BLOCK), 1)
    band = (kcol > qrow) & (kcol <= qrow + ATTN_BLOCK)
    lane = lax.broadcasted_iota(jnp.int32, (1, LANES), 1)
    low = lane < HEAD_DIM

    ones_lo = jnp.broadcast_to(jnp.where(low, 1.0, 0.0), (2 * ATTN_BLOCK, LANES)).astype(BF16)
    ones_hi = jnp.broadcast_to(jnp.where(low, 0.0, 1.0), (2 * ATTN_BLOCK, LANES)).astype(BF16)

    def attn_block(n, carry):
        row0 = pl.multiple_of(n * ATTN_BLOCK, ATTN_BLOCK)
        rows = pl.ds(row0, ATTN_BLOCK)
        keys = pl.ds(row0, 2 * ATTN_BLOCK)
        valid = band & ((kcol >= ATTN_BLOCK) | (i > 0) | (n > 0))
        for g in range(N_KV_HEADS):
            cols = slice(g * LANES, (g + 1) * LANES)
            kk = jnp.concatenate([k0_ref[keys, cols], k1_ref[keys, cols]], axis=0)
            vv = jnp.concatenate(
                [jnp.concatenate([v0_ref[keys, cols], ones_lo], axis=1),
                 jnp.concatenate([v1_ref[keys, cols], ones_hi], axis=1)], axis=0)
            pair0 = g * PAIRS_PER_KV
            qg = jnp.concatenate(
                [q_ref[rows, (pair0 + p) * LANES:(pair0 + p + 1) * LANES]
                 for p in range(PAIRS_PER_KV)], axis=0)
            s_all = lax.dot_general(qg, kk, (((1,), (1,)), ((), ())),
                                    preferred_element_type=F32)
            pbuf = p_ref.at[g]
            sink_terms = []
            for p in range(PAIRS_PER_KV):
                for e in range(HEADS_PER_VREG):
                    head = (pair0 + p) * HEADS_PER_VREG + e
                    sink = sinks_ref[head] * LOG2_E
                    s = s_all[p * ATTN_BLOCK:(p + 1) * ATTN_BLOCK,
                              e * 2 * ATTN_BLOCK:(e + 1) * 2 * ATTN_BLOCK]
                    s = jnp.where(valid, s, NEG_INF)
                    m = jnp.maximum(jnp.max(s, axis=-1, keepdims=True), sink)
                    sink_terms.append(jnp.exp2(sink - m))
                    pbuf[p * ATTN_BLOCK:(p + 1) * ATTN_BLOCK,
                         e * 2 * ATTN_BLOCK:(e + 1) * 2 * ATTN_BLOCK] = jnp.exp2(s - m).astype(BF16)
            o_all = _dot(pbuf[...], vv)
            for p in range(PAIRS_PER_KV):
                blk = o_all[p * ATTN_BLOCK:(p + 1) * ATTN_BLOCK, :]
                denom = blk[:, LANES:] + jnp.where(low, sink_terms[2 * p], sink_terms[2 * p + 1])
                o_ref[rows, (pair0 + p) * LANES:(pair0 + p + 1) * LANES] = blk[:, :LANES] / denom
        return carry

    lax.fori_loop(0, tq // ATTN_BLOCK, attn_block, 0)


    z = _dot(xb, wqg_ref[:, D_MODEL:])
    gated = (o_ref[...] * _silu(z)).astype(BF16)
    r = DEEPNORM_ALPHA * x1 + _dot(gated, wout_ref[...])
    out_ref[0] = _layer_norm(r, g_ref[...], b_ref[...])


def _resident(shape):
    return pl.BlockSpec(shape, lambda *_: (0,) * len(shape), pipeline_mode=pl.Buffered(1))


def _rope_tables(seq):
    inv_freq = ROPE_THETA ** (-jnp.arange(0, HEAD_DIM, 2, dtype=F32) / HEAD_DIM)
    ang = jnp.arange(seq, dtype=F32)[:, None] * inv_freq[None, :]
    ang = jnp.concatenate([ang, ang], axis=-1)
    sign = jnp.concatenate([-jnp.ones((HALF_DIM,), F32), jnp.ones((HALF_DIM,), F32)])
    cos = jnp.tile(jnp.cos(ang), (1, HEADS_PER_VREG))
    sin_signed = jnp.tile(jnp.sin(ang) * sign, (1, HEADS_PER_VREG))
    return cos, sin_signed


def _layer_a(x, w_in, w_group, scale, w_out, ln_g, ln_b, w_kv, cos, sin_signed):
    batch, seq, _ = x.shape
    tm = TILE_A
    tok = lambda width: pl.BlockSpec((1, tm, width), lambda b, i: (b, i, 0))
    table = pl.BlockSpec((tm, LANES), lambda b, i: (i, 0))
    return pl.pallas_call(
        _layer_a_kernel,
        grid=(batch, seq // tm),
        in_specs=[
            tok(D_MODEL),
            _resident((D_MODEL, 2 * D_MODEL)),
            _resident((N_POOL_GROUPS, POOL_GROUP_DIM, POOL_GROUP_DIM)),
            _resident((1, D_MODEL)),
            _resident((D_MODEL, D_MODEL)),
            _resident((1, D_MODEL)),
            _resident((1, D_MODEL)),
            _resident((D_MODEL, 2 * KV_WIDTH)),
            table, table,
        ],
        out_specs=[tok(D_MODEL), tok(KV_WIDTH), tok(KV_WIDTH)],
        out_shape=[
            jax.ShapeDtypeStruct((batch, seq, D_MODEL), F32),
            jax.ShapeDtypeStruct((batch, seq, KV_WIDTH), BF16),
            jax.ShapeDtypeStruct((batch, seq, KV_WIDTH), BF16),
        ],
        scratch_shapes=[
            pltpu.VMEM((MAX_WINDOW, D_MODEL), F32),
            pltpu.VMEM((tm, D_MODEL), BF16),
        ],
        compiler_params=pltpu.CompilerParams(
            dimension_semantics=("arbitrary", "arbitrary"),
            vmem_limit_bytes=VMEM_LIMIT_BYTES),
        name="yoco_pool_layer",
    )(x, w_in, w_group, scale, w_out, ln_g, ln_b, w_kv, cos, sin_signed)


def _layer_b(x1, k, v, w_qg, sinks, w_out, ln_g, ln_b, cos, sin_signed):
    batch, seq, _ = x1.shape
    tq = TILE_B
    blocks_per_tile = tq // ATTN_BLOCK
    tok = lambda width: pl.BlockSpec((1, tq, width), lambda b, i: (b, i, 0))
    prev = pl.BlockSpec((1, ATTN_BLOCK, KV_WIDTH),
                        lambda b, i: (b, jnp.maximum(i * blocks_per_tile - 1, 0), 0))
    table = pl.BlockSpec((tq, LANES), lambda b, i: (i, 0))
    padded = N_KV_HEADS * LANES
    return pl.pallas_call(
        _layer_b_kernel,
        grid=(batch, seq // tq),
        in_specs=[
            pl.BlockSpec(memory_space=pltpu.SMEM),
            tok(D_MODEL),
            tok(KV_WIDTH), prev, tok(KV_WIDTH), prev,
            _resident((D_MODEL, 2 * D_MODEL)),
            _resident((D_MODEL, D_MODEL)),
            _resident((1, D_MODEL)),
            _resident((1, D_MODEL)),
            table, table,
        ],
        out_specs=tok(D_MODEL),
        out_shape=jax.ShapeDtypeStruct((batch, seq, D_MODEL), F32),
        scratch_shapes=[
            pltpu.VMEM((tq, D_MODEL), BF16),
            pltpu.VMEM((tq + ATTN_BLOCK, padded), BF16),
            pltpu.VMEM((tq + ATTN_BLOCK, padded), BF16),
            pltpu.VMEM((tq + ATTN_BLOCK, padded), BF16),
            pltpu.VMEM((tq + ATTN_BLOCK, padded), BF16),
            pltpu.VMEM((N_KV_HEADS,
                        PAIRS_PER_KV * ATTN_BLOCK, HEADS_PER_VREG * 2 * ATTN_BLOCK), BF16),
            pltpu.VMEM((tq, D_MODEL), F32),
        ],
        compiler_params=pltpu.CompilerParams(
            dimension_semantics=("arbitrary", "arbitrary"),
            vmem_limit_bytes=VMEM_LIMIT_BYTES),
        name="yoco_swa_layer",
    )(sinks, x1, k, k, v, v, w_qg, w_out, ln_g, ln_b, cos, sin_signed)


def kernel(x, ln_g, ln_b, a_w_in, a_w_group, a_scale, a_w_out, b_w_k, b_w_v, b_w_qg, b_sinks, b_w_out):
    assert a_w_in.shape[0] == 1 and b_w_qg.shape[0] == 1
    seq = x.shape[1]
    cos, sin_signed = _rope_tables(seq)
    w_kv = jnp.concatenate([b_w_k, b_w_v], axis=1).astype(BF16)
    x1, k, v = _layer_a(
        x, a_w_in[0].astype(BF16), a_w_group[0].astype(BF16), a_scale[0][None, :],
        a_w_out[0].astype(BF16), ln_g[0][None, :], ln_b[0][None, :], w_kv, cos, sin_signed)
    return _layer_b(
        x1, k, v, b_w_qg[0].astype(BF16), b_sinks[0], b_w_out[0].astype(BF16),
        ln_g[1][None, :], ln_b[1][None, :], cos, sin_signed)
```

```python
import functools

import jax
import jax.numpy as jnp
from jax import lax
from jax.experimental import pallas as pl
from jax.experimental.pallas import tpu as pltpu

D_MODEL = 2048
DEPTH = 2
POOL_WINDOWS = (2, 4, 8, 16)
N_POOL_GROUPS = len(POOL_WINDOWS)
POOL_GROUP_DIM = D_MODEL // N_POOL_GROUPS
MAX_WINDOW = max(POOL_WINDOWS)
HEAD_DIM = 64
HALF_DIM = HEAD_DIM // 2
N_Q_HEADS = D_MODEL // HEAD_DIM
N_KV_HEADS = N_Q_HEADS // 8
GQA_GROUP = N_Q_HEADS // N_KV_HEADS
KV_WIDTH = N_KV_HEADS * HEAD_DIM
ATTN_BLOCK = 128
ROPE_THETA = 10000.0
LN_EPS = 1e-5
NEG_INF = -1e30
LOG2_E = 1.4426950408889634
DEEPNORM_ALPHA = (2 * DEPTH) ** 0.25

LANES = 128
HEADS_PER_VREG = LANES // HEAD_DIM
PAIRS_PER_KV = GQA_GROUP // HEADS_PER_VREG
VMEM_LIMIT_BYTES = 56 * 1024 * 1024
TILE_A = 256
TILE_B = 256

F32 = jnp.float32
BF16 = jnp.bfloat16


def _dot(a, b):
    return jnp.dot(a, b, preferred_element_type=F32)


def _silu(z):
    return z / (1.0 + jnp.exp(-z))


def _layer_norm(r, g, b):
    mu = jnp.mean(r, axis=-1, keepdims=True)
    c = r - mu
    var = jnp.mean(c * c, axis=-1, keepdims=True)
    return c * lax.rsqrt(var + LN_EPS) * g + b


def _rope(t, cos, sin_signed):
    width = t.shape[1]
    reps = width // LANES
    cos_w = jnp.concatenate([cos] * reps, axis=1)
    sin_w = jnp.concatenate([sin_signed] * reps, axis=1)
    lane = lax.broadcasted_iota(jnp.int32, (1, width), 1)
    first_half = (lane % HEAD_DIM) < HALF_DIM
    partner = jnp.where(first_half,
                        pltpu.roll(t, width - HALF_DIM, axis=1),
                        pltpu.roll(t, HALF_DIM, axis=1))
    return t * cos_w + partner * sin_w


def _layer_a_kernel(x_ref, win_ref, wg_ref, scale_ref, wout_ref, g_ref, b_ref, wkv_ref,
                    cos_ref, sin_ref, x1_ref, k_ref, v_ref, carry_ref, y_ref, r_ref,
                    *, n_tiles, tiles_per_seq):
    tm = x_ref.shape[1]
    t = pl.program_id(0)

    @pl.when(t == 0)
    def _():
        r_ref[...] = jnp.zeros_like(r_ref)
        carry_ref[...] = jnp.zeros_like(carry_ref)

    def finish_previous_tile():
        x1 = _layer_norm(r_ref[...], g_ref[...], b_ref[...])
        x1_ref[0] = x1
        kv = _dot(x1.astype(BF16), wkv_ref[...])
        k = _rope(kv[:, :KV_WIDTH], cos_ref[...], sin_ref[...])
        k_ref[0] = k.astype(BF16)
        v_ref[0] = kv[:, KV_WIDTH:].astype(BF16)

    def mix_this_tile():
        i = t % tiles_per_seq
        x = x_ref[0]
        xb = x.astype(BF16)
        pos1 = lax.broadcasted_iota(jnp.int32, (tm, 1), 0) + (i * tm + 1)
        for g, w in enumerate(POOL_WINDOWS):
            lo, hi = g * POOL_GROUP_DIM, (g + 1) * POOL_GROUP_DIM
            u = _dot(xb, win_ref[:, lo:hi])
            z = _dot(xb, win_ref[:, D_MODEL + lo:D_MODEL + hi])
            history = jnp.where(i > 0, carry_ref[:, lo:hi], 0.0)
            acc = jnp.concatenate([history, u], axis=0)
            carry_ref[:, lo:hi] = u[tm - MAX_WINDOW:, :]
            shift = 1
            while shift < w:
                acc = acc + pltpu.roll(acc, shift, axis=0)
                shift *= 2
            inv_count = 1.0 / jnp.minimum(pos1, w).astype(F32)
            pooled = acc[MAX_WINDOW:, :] * inv_count - u
            mixed = _dot(pooled.astype(BF16), wg_ref[g])
            y = mixed * scale_ref[:, lo:hi] * _silu(z)
            y_ref[:, lo:hi] = y.astype(BF16)
        r_ref[...] = DEEPNORM_ALPHA * x + _dot(y_ref[...], wout_ref[...])

    @pl.when(t < n_tiles)
    def _():
        finish_previous_tile()
        mix_this_tile()

    @pl.when(t == n_tiles)
    def _():
        finish_previous_tile()


def _pad_heads(t):
    lane = lax.broadcasted_iota(jnp.int32, (1, LANES), 1)
    low = lane < HEAD_DIM
    lo_cols, hi_cols = [], []
    for j in range(KV_WIDTH // LANES):
        col = t[:, j * LANES:(j + 1) * LANES]
        swapped = pltpu.roll(col, HEAD_DIM, axis=1)
        lo_cols += [jnp.where(low, col, 0.0), jnp.where(low, swapped, 0.0)]
        hi_cols += [jnp.where(low, 0.0, swapped), jnp.where(low, 0.0, col)]
    return (jnp.concatenate(lo_cols, axis=1).astype(BF16),
            jnp.concatenate(hi_cols, axis=1).astype(BF16))


def _layer_b_kernel(sinks_ref, x1_ref, kc_ref, kp_ref, vc_ref, vp_ref, wqg_ref, wout_ref,
                    g_ref, b_ref, cos_ref, sin_ref, out_ref,
                    q_ref, k0_ref, k1_ref, v0_ref, v1_ref, p_ref, o_ref, gate_ref, r_ref,
                    *, n_tiles, tiles_per_seq):
    tq = x1_ref.shape[1]
    t = pl.program_id(0)

    @pl.when(t == 0)
    def _():
        r_ref[...] = jnp.zeros_like(r_ref)

    def finish_previous_tile():
        out_ref[0] = _layer_norm(r_ref[...], g_ref[...], b_ref[...])

    def attend_this_tile():
        i = t % tiles_per_seq
        x1 = x1_ref[0]
        xb = x1.astype(BF16)

        q = _rope(_dot(xb, wqg_ref[:, :D_MODEL]), cos_ref[...], sin_ref[...])
        q_ref[...] = (q * (HEAD_DIM ** -0.5 * LOG2_E)).astype(BF16)

        kf = jnp.concatenate([kp_ref[0], kc_ref[0]], axis=0).astype(F32)
        vf = jnp.concatenate([vp_ref[0], vc_ref[0]], axis=0).astype(F32)
        k0_ref[...], k1_ref[...] = _pad_heads(kf)
        v0_ref[...], v1_ref[...] = _pad_heads(vf)

        gate_ref[...] = _silu(_dot(xb, wqg_ref[:, D_MODEL:]))

        qrow = lax.broadcasted_iota(jnp.int32, (ATTN_BLOCK, 2 * ATTN_BLOCK), 0)
        kcol = lax.broadcasted_iota(jnp.int32, (ATTN_BLOCK, 2 * ATTN_BLOCK), 1)
        band = (kcol > qrow) & (kcol <= qrow + ATTN_BLOCK)
        lane = lax.broadcasted_iota(jnp.int32, (1, LANES), 1)
        low = lane < HEAD_DIM

        ones_lo = jnp.broadcast_to(jnp.where(low, 1.0, 0.0), (2 * ATTN_BLOCK, LANES)).astype(BF16)
        ones_hi = jnp.broadcast_to(jnp.where(low, 0.0, 1.0), (2 * ATTN_BLOCK, LANES)).astype(BF16)
        key_row = lax.broadcasted_iota(jnp.int32, (2 * ATTN_BLOCK, LANES), 0)
        keep = jnp.where(key_row == 0, 0.0, 1.0).astype(BF16)

        def attn_block(n):
            rows = pl.ds(n * ATTN_BLOCK, ATTN_BLOCK)
            keys = pl.ds(n * ATTN_BLOCK, 2 * ATTN_BLOCK)
            valid_lo = band[:, :LANES] if n > 0 else band[:, :LANES] & (i > 0)
            valid_hi = band[:, LANES:]
            for g in range(N_KV_HEADS):
                cols = slice(g * LANES, (g + 1) * LANES)
                vv = jnp.concatenate(
                    [jnp.concatenate([v0_ref[keys, cols] * keep, ones_lo], axis=1),
                     jnp.concatenate([v1_ref[keys, cols] * keep, ones_hi], axis=1)], axis=0)
                pair0 = g * PAIRS_PER_KV
                qg = jnp.concatenate(
                    [q_ref[rows, (pair0 + p) * LANES:(pair0 + p + 1) * LANES]
                     for p in range(PAIRS_PER_KV)], axis=0)
                pbuf = p_ref.at[n * N_KV_HEADS + g]
                for e, kpad_ref in enumerate((k0_ref, k1_ref)):
                    s_e = lax.dot_general(qg, kpad_ref[keys, cols], (((1,), (1,)), ((), ())),
                                          preferred_element_type=F32)
                    for p in range(PAIRS_PER_KV):
                        head = (pair0 + p) * HEADS_PER_VREG + e
                        sink_fill = jnp.where(lane == 0, sinks_ref[head] * LOG2_E, NEG_INF)
                        r0, c0 = p * ATTN_BLOCK, e * 2 * ATTN_BLOCK
                        s_lo = jnp.where(valid_lo, s_e[r0:r0 + ATTN_BLOCK, :LANES], sink_fill)
                        s_hi = jnp.where(valid_hi, s_e[r0:r0 + ATTN_BLOCK, LANES:], NEG_INF)
                        m = jnp.max(jnp.maximum(s_lo, s_hi), axis=-1, keepdims=True)
                        pbuf[r0:r0 + ATTN_BLOCK, c0:c0 + LANES] = jnp.exp2(s_lo - m).astype(BF16)
                        pbuf[r0:r0 + ATTN_BLOCK, c0 + LANES:c0 + 2 * LANES] = (
                            jnp.exp2(s_hi - m).astype(BF16))
                o_all = _dot(pbuf[...], vv)
                for p in range(PAIRS_PER_KV):
                    blk = o_all[p * ATTN_BLOCK:(p + 1) * ATTN_BLOCK, :]
                    o_ref[rows, (pair0 + p) * LANES:(pair0 + p + 1) * LANES] = (
                        blk[:, :LANES] / blk[:, LANES:])

        @pl.when(i >= 0)
        def _():
            for n in range(tq // ATTN_BLOCK):
                attn_block(n)

        finish_previous_tile()
        gated = (o_ref[...] * gate_ref[...]).astype(BF16)
        r_ref[...] =DEEPNORM_ALPHA * x1 + _dot(gated, wout_ref[...])

    @pl.when(t < n_tiles)
    def _():
        attend_this_tile()

    @pl.when(t == n_tiles)
    def _():
        finish_previous_tile()


def _resident(shape):
    return pl.BlockSpec(shape, lambda *_: (0,) * len(shape), pipeline_mode=pl.Buffered(1))


def _rope_tables(seq):
    inv_freq = ROPE_THETA ** (-jnp.arange(0, HEAD_DIM, 2, dtype=F32) / HEAD_DIM)
    ang = jnp.arange(seq, dtype=F32)[:, None] * inv_freq[None, :]
    ang = jnp.concatenate([ang, ang], axis=-1)
    sign = jnp.concatenate([-jnp.ones((HALF_DIM,), F32), jnp.ones((HALF_DIM,), F32)])
    cos = jnp.tile(jnp.cos(ang), (1, HEADS_PER_VREG))
    sin_signed = jnp.tile(jnp.sin(ang) * sign, (1, HEADS_PER_VREG))
    return cos, sin_signed


def _tile_maps(n_tiles, tiles_per_seq):
    def split(tile):
        return tile // tiles_per_seq, tile % tiles_per_seq

    def this_tile(t):
        return split(jnp.minimum(t, n_tiles - 1))

    def previous_tile(t):
        return split(jnp.maximum(t - 1, 0))

    return this_tile, previous_tile


def _layer_a(x, w_in, w_group, scale, w_out, ln_g, ln_b, w_kv, cos, sin_signed):
    batch, seq, _ = x.shape
    tm = TILE_A
    tiles_per_seq = seq // tm
    n_tiles = batch * tiles_per_seq
    this_tile, previous_tile = _tile_maps(n_tiles, tiles_per_seq)
    lagged = lambda width: pl.BlockSpec((1, tm, width), lambda t: (*previous_tile(t), 0))
    lagged_table = pl.BlockSpec((tm, LANES), lambda t: (previous_tile(t)[1], 0))
    return pl.pallas_call(
        functools.partial(_layer_a_kernel, n_tiles=n_tiles, tiles_per_seq=tiles_per_seq),
        grid=(n_tiles + 1,),
        in_specs=[
            pl.BlockSpec((1, tm, D_MODEL), lambda t: (*this_tile(t), 0)),
            _resident((D_MODEL, 2 * D_MODEL)),
            _resident((N_POOL_GROUPS, POOL_GROUP_DIM, POOL_GROUP_DIM)),
            _resident((1, D_MODEL)),
            _resident((D_MODEL, D_MODEL)),
            _resident((1, D_MODEL)),
            _resident((1, D_MODEL)),
            _resident((D_MODEL, 2 * KV_WIDTH)),
            lagged_table, lagged_table,
        ],
        out_specs=[lagged(D_MODEL), lagged(KV_WIDTH), lagged(KV_WIDTH)],
        out_shape=[
            jax.ShapeDtypeStruct((batch, seq, D_MODEL), F32),
            jax.ShapeDtypeStruct((batch, seq, KV_WIDTH), BF16),
            jax.ShapeDtypeStruct((batch, seq, KV_WIDTH), BF16),
        ],
        scratch_shapes=[
            pltpu.VMEM((MAX_WINDOW, D_MODEL), F32),
            pltpu.VMEM((tm, D_MODEL), BF16),
            pltpu.VMEM((tm, D_MODEL), F32),
        ],
        compiler_params=pltpu.CompilerParams(
            dimension_semantics=("arbitrary",),
            vmem_limit_bytes=VMEM_LIMIT_BYTES),
        name="yoco_pool_layer",
    )(x, w_in, w_group, scale, w_out, ln_g, ln_b, w_kv, cos, sin_signed)


def _layer_b(x1, k, v, w_qg, sinks, w_out, ln_g, ln_b, cos, sin_signed):
    batch, seq, _ = x1.shape
    tq = TILE_B
    blocks_per_tile = tq // ATTN_BLOCK
    tiles_per_seq = seq // tq
    n_tiles = batch * tiles_per_seq
    this_tile, previous_tile = _tile_maps(n_tiles, tiles_per_seq)
    tok = lambda width: pl.BlockSpec((1, tq, width), lambda t: (*this_tile(t), 0))

    def previous_block(t):
        b, i = this_tile(t)
        return b, jnp.maximum(i * blocks_per_tile - 1, 0), 0

    prev = pl.BlockSpec((1, ATTN_BLOCK, KV_WIDTH), previous_block)
    table = pl.BlockSpec((tq, LANES), lambda t: (this_tile(t)[1], 0))
    padded = N_KV_HEADS * LANES
    return pl.pallas_call(
        functools.partial(_layer_b_kernel, n_tiles=n_tiles, tiles_per_seq=tiles_per_seq),
        grid=(n_tiles + 1,),
        in_specs=[
            pl.BlockSpec(memory_space=pltpu.SMEM),
            tok(D_MODEL),
            tok(KV_WIDTH), prev, tok(KV_WIDTH), prev,
            _resident((D_MODEL, 2 * D_MODEL)),
            _resident((D_MODEL, D_MODEL)),
            _resident((1, D_MODEL)),
            _resident((1, D_MODEL)),
            table, table,
        ],
        out_specs=pl.BlockSpec((1, tq, D_MODEL), lambda t: (*previous_tile(t), 0)),
        out_shape=jax.ShapeDtypeStruct((batch, seq, D_MODEL), F32),
        scratch_shapes=[
            pltpu.VMEM((tq, D_MODEL), BF16),
            pltpu.VMEM((tq + ATTN_BLOCK, padded), BF16),
            pltpu.VMEM((tq + ATTN_BLOCK, padded), BF16),
            pltpu.VMEM((tq + ATTN_BLOCK, padded), BF16),
            pltpu.VMEM((tq + ATTN_BLOCK, padded), BF16),
            pltpu.VMEM((blocks_per_tile * N_KV_HEADS,
                        PAIRS_PER_KV * ATTN_BLOCK, HEADS_PER_VREG * 2 * ATTN_BLOCK), BF16),
            pltpu.VMEM((tq, D_MODEL), F32),
            pltpu.VMEM((tq, D_MODEL), F32),
            pltpu.VMEM((tq, D_MODEL), F32),
        ],
        compiler_params=pltpu.CompilerParams(
            dimension_semantics=("arbitrary",),
            vmem_limit_bytes=VMEM_LIMIT_BYTES),
        name="yoco_swa_layer",
    )(sinks, x1, k, k, v, v, w_qg, w_out, ln_g, ln_b, cos, sin_signed)


def kernel(x, ln_g, ln_b, a_w_in, a_w_group, a_scale, a_w_out, b_w_k, b_w_v, b_w_qg, b_sinks, b_w_out):
    assert a_w_in.shape[0] == 1 and b_w_qg.shape[0] == 1
    seq = x.shape[1]
    cos, sin_signed = _rope_tables(seq)
    w_kv = jnp.concatenate([b_w_k, b_w_v], axis=1).astype(BF16)
    x1, k, v = _layer_a(
        x, a_w_in[0].astype(BF16), a_w_group[0].astype(BF16), a_scale[0][None, :],
        a_w_out[0].astype(BF16), ln_g[0][None, :], ln_b[0][None, :], w_kv, cos, sin_signed)
    return _layer_b(
        x1, k, v, b_w_qg[0].astype(BF16), b_sinks[0], b_w_out[0].astype(BF16),
        ln_g[1][None, :], ln_b[1][None, :], cos, sin_signed)
```

```python
import functools

import jax
import jax.numpy as jnp
from jax import lax
from jax.experimental import pallas as pl
from jax.experimental.pallas import tpu as pltpu

D_MODEL = 2048
DEPTH = 2
POOL_WINDOWS = (2, 4, 8, 16)
N_POOL_GROUPS = len(POOL_WINDOWS)
POOL_GROUP_DIM = D_MODEL // N_POOL_GROUPS
MAX_WINDOW = max(POOL_WINDOWS)
HEAD_DIM = 64
HALF_DIM = HEAD_DIM // 2
N_Q_HEADS = D_MODEL // HEAD_DIM
N_KV_HEADS = N_Q_HEADS // 8
GQA_GROUP = N_Q_HEADS // N_KV_HEADS
KV_WIDTH = N_KV_HEADS * HEAD_DIM
ATTN_BLOCK = 128
ROPE_THETA = 10000.0
LN_EPS = 1e-5
NEG_INF = -1e30
LOG2_E = 1.4426950408889634
DEEPNORM_ALPHA = (2 * DEPTH) ** 0.25

LANES = 128
BF16_SUBLANES = 16
HEADS_PER_VREG = LANES // HEAD_DIM
PAIRS_PER_KV = GQA_GROUP // HEADS_PER_VREG
VMEM_LIMIT_BYTES = 56 * 1024 * 1024
TILE_A = 256
TILE_B = 256

F32 = jnp.float32
BF16 = jnp.bfloat16


def _dot(a, b):
    return jnp.dot(a, b, preferred_element_type=F32)


def _silu(z):
    return z / (1.0 + jnp.exp(-z))


def _layer_norm(r, g, b):
    mu = jnp.mean(r, axis=-1, keepdims=True)
    c = r - mu
    var = jnp.mean(c * c, axis=-1, keepdims=True)
    return c * lax.rsqrt(var + LN_EPS) * g + b


def _rope(t, cos, sin_signed):
    width = t.shape[1]
    reps = width // LANES
    cos_w = jnp.concatenate([cos] * reps, axis=1)
    sin_w = jnp.concatenate([sin_signed] * reps, axis=1)
    lane = lax.broadcasted_iota(jnp.int32, (1, width), 1)
    first_half = (lane % HEAD_DIM) < HALF_DIM
    partner = jnp.where(first_half,
                        pltpu.roll(t, width - HALF_DIM, axis=1),
                        pltpu.roll(t, HALF_DIM, axis=1))
    return t * cos_w + partner * sin_w


def _layer_a_kernel(x_ref, win_ref, wg_ref, scale_ref, wout_ref, g_ref, b_ref, wkv_ref,
                    cos_ref, sin_ref, next_wqg_ref, next_wout_ref,
                    x1_ref, k_ref, v_ref, next_wqg_bf_ref, next_wout_bf_ref,
                    carry_ref, y_ref, r_ref, *, n_tiles, tiles_per_seq):
    tm = x_ref.shape[1]
    t = pl.program_id(0)

    next_wqg_bf_ref[...] = next_wqg_ref[...].astype(BF16)
    next_wout_bf_ref[...] = next_wout_ref[...].astype(BF16)

    @pl.when(t == 0)
    def _():
        r_ref[...] = jnp.zeros_like(r_ref)
        carry_ref[...] = jnp.zeros_like(carry_ref)

    def finish_previous_tile():
        x1 = _layer_norm(r_ref[...], g_ref[...], b_ref[...])
        x1_ref[0] = x1
        kv = _dot(x1.astype(BF16), wkv_ref[...])
        k = _rope(kv[:, :KV_WIDTH], cos_ref[...], sin_ref[...])
        k_ref[0] = k.astype(BF16)
        v_ref[0] = kv[:, KV_WIDTH:].astype(BF16)

    def mix_this_tile():
        i = t % tiles_per_seq
        x = x_ref[0]
        xb = x.astype(BF16)
        pos1 = lax.broadcasted_iota(jnp.int32, (tm, 1), 0) + (i * tm + 1)
        for g, w in enumerate(POOL_WINDOWS):
            lo, hi = g * POOL_GROUP_DIM, (g + 1) * POOL_GROUP_DIM
            u = _dot(xb, win_ref[:, lo:hi])
            z = _dot(xb, win_ref[:, D_MODEL + lo:D_MODEL + hi])
            history = jnp.where(i > 0, carry_ref[:, lo:hi], 0.0)
            acc = jnp.concatenate([history, u], axis=0)
            carry_ref[:, lo:hi] = u[tm - MAX_WINDOW:, :]
            shift = 1
            while shift < w:
                acc = acc + pltpu.roll(acc, shift, axis=0)
                shift *= 2
            inv_count = 1.0 / jnp.minimum(pos1, w).astype(F32)
            pooled = acc[MAX_WINDOW:, :] * inv_count - u
            mixed = _dot(pooled.astype(BF16), wg_ref[g])
            y = mixed * scale_ref[:, lo:hi] * _silu(z)
            y_ref[:, lo:hi] = y.astype(BF16)
        r_ref[...] = DEEPNORM_ALPHA * x + _dot(y_ref[...], wout_ref[...])

    @pl.when(t < n_tiles)
    def _():
        finish_previous_tile()
        mix_this_tile()

    @pl.when(t == n_tiles)
    def _():
        finish_previous_tile()


def _pad_heads(t):
    lane = lax.broadcasted_iota(jnp.int32, (1, LANES), 1)
    low = lane < HEAD_DIM
    lo_cols, hi_cols = [], []
    for j in range(KV_WIDTH // LANES):
        col = t[:, j * LANES:(j + 1) * LANES]
        swapped = pltpu.roll(col, HEAD_DIM, axis=1)
        lo_cols += [jnp.where(low, col, 0.0), jnp.where(low, swapped, 0.0)]
        hi_cols += [jnp.where(low, 0.0, swapped), jnp.where(low, 0.0, col)]
    return (jnp.concatenate(lo_cols, axis=1).astype(BF16),
            jnp.concatenate(hi_cols, axis=1).astype(BF16))


def _layer_b_kernel(sinks_ref, x1_ref, kc_ref, kp_ref, vc_ref, vp_ref, wqg_ref, wout_ref,
                    g_ref, b_ref, cos_ref, sin_ref, out_ref,
                    q_ref, k0_ref, k1_ref, v0_ref, v1_ref, p_ref, o_ref, gate_ref, r_ref,
                    *, n_tiles, tiles_per_seq):
    tq = x1_ref.shape[1]
    t = pl.program_id(0)

    @pl.when(t == 0)
    def _():
        r_ref[...] = jnp.zeros_like(r_ref)

    def finish_previous_tile():
        out_ref[0] = _layer_norm(r_ref[...], g_ref[...], b_ref[...])

    def attend_this_tile():
        i = t % tiles_per_seq
        x1 = x1_ref[0]
        xb = x1.astype(BF16)

        q = _rope(_dot(xb, wqg_ref[:, :D_MODEL]), cos_ref[...], sin_ref[...])
        q_ref[...] = (q * (HEAD_DIM ** -0.5 * LOG2_E)).astype(BF16)

        kf = jnp.concatenate([kp_ref[0], kc_ref[0]], axis=0).astype(F32)
        vf = jnp.concatenate([vp_ref[0], vc_ref[0]], axis=0).astype(F32)
        k0_ref[...], k1_ref[...] = _pad_heads(kf)
        v0_ref[...], v1_ref[...] = _pad_heads(vf)

        gate_ref[...] = _silu(_dot(xb, wqg_ref[:, D_MODEL:]))

        qrow = lax.broadcasted_iota(jnp.int32, (ATTN_BLOCK, 2 * ATTN_BLOCK), 0)
        kcol = lax.broadcasted_iota(jnp.int32, (ATTN_BLOCK, 2 * ATTN_BLOCK), 1)
        band = (kcol > qrow) & (kcol <= qrow + ATTN_BLOCK)
        lane = lax.broadcasted_iota(jnp.int32, (1, LANES), 1)
        low = lane < HEAD_DIM

        ones_lo = jnp.broadcast_to(jnp.where(low, 1.0, 0.0), (2 * ATTN_BLOCK, LANES)).astype(BF16)
        ones_hi = jnp.broadcast_to(jnp.where(low, 0.0, 1.0), (2 * ATTN_BLOCK, LANES)).astype(BF16)
        key_row = lax.broadcasted_iota(jnp.int32, (2 * ATTN_BLOCK, LANES), 0)
        keep = jnp.where(key_row == 0, 0.0, 1.0).astype(BF16)

        def attn_block(n):
            rows = pl.ds(n * ATTN_BLOCK, ATTN_BLOCK)
            keys = pl.ds(n * ATTN_BLOCK, 2 * ATTN_BLOCK)
            valid_lo = band[:, :LANES] if n > 0 else band[:, :LANES] & (i > 0)
            valid_hi = band[:, LANES:]
            for g in range(N_KV_HEADS):
                cols = slice(g * LANES, (g + 1) * LANES)
                vv = jnp.concatenate(
                    [jnp.concatenate([v0_ref[keys, cols] * keep, ones_lo], axis=1),
                     jnp.concatenate([v1_ref[keys, cols] * keep, ones_hi], axis=1)], axis=0)
                pair0 = g * PAIRS_PER_KV
                qg = jnp.concatenate(
                    [q_ref[rows, (pair0 + p) * LANES:(pair0 + p + 1) * LANES]
                     for p in range(PAIRS_PER_KV)], axis=0)
                pbuf = p_ref.at[n * N_KV_HEADS + g]
                for e, kpad_ref in enumerate((k0_ref, k1_ref)):
                    s_e = lax.dot_general(qg, kpad_ref[keys, cols], (((1,), (1,)), ((), ())),
                                          preferred_element_type=F32)
                    for p in range(PAIRS_PER_KV):
                        head = (pair0 + p) * HEADS_PER_VREG + e
                        sink_fill = jnp.where(lane == 0, sinks_ref[head] * LOG2_E, NEG_INF)
                        r0, c0 = p * ATTN_BLOCK, e * 2 * ATTN_BLOCK
                        s_lo = jnp.where(valid_lo, s_e[r0:r0 + ATTN_BLOCK, :LANES], sink_fill)
                        s_hi = jnp.where(valid_hi, s_e[r0:r0 + ATTN_BLOCK, LANES:], NEG_INF)
                        m = jnp.max(jnp.maximum(s_lo, s_hi), axis=-1, keepdims=True)
                        pbuf[r0:r0 + ATTN_BLOCK, c0:c0 + LANES] = jnp.exp2(s_lo - m).astype(BF16)
                        pbuf[r0:r0 + ATTN_BLOCK, c0 + LANES:c0 + 2 * LANES] = (
                            jnp.exp2(s_hi - m).astype(BF16))
                o_all = _dot(pbuf[...], vv)
                for p in range(PAIRS_PER_KV):
                    blk = o_all[p * ATTN_BLOCK:(p + 1) * ATTN_BLOCK, :]
                    o_ref[rows, (pair0 + p) * LANES:(pair0 + p + 1) * LANES] = (
                        blk[:, :LANES] / blk[:, LANES:])

        @pl.when(i >= 0)
        def _():
            for n in range(tq // ATTN_BLOCK):
                attn_block(n)

        finish_previous_tile()
        gated = (o_ref[...] * gate_ref[...]).astype(BF16)
        r_ref[...] =DEEPNORM_ALPHA * x1 + _dot(gated, wout_ref[...])

    @pl.when(t < n_tiles)
    def _():
        attend_this_tile()

    @pl.when(t == n_tiles)
    def _():
        finish_previous_tile()


def _resident(shape):
    return pl.BlockSpec(shape, lambda *_: (0,) * len(shape), pipeline_mode=pl.Buffered(1))


def _rope_tables(seq):
    lane = jnp.arange(LANES)
    inv_freq = ROPE_THETA ** (-(2 * (lane % HALF_DIM)).astype(F32) / HEAD_DIM)
    ang = jnp.arange(seq, dtype=F32)[:, None] * inv_freq[None, :]
    sign = jnp.where((lane % HEAD_DIM) < HALF_DIM, -1.0, 1.0).astype(F32)
    return jnp.cos(ang), jnp.sin(ang) * sign[None, :]


def _tile_maps(n_tiles, tiles_per_seq):
    def split(tile):
        return tile // tiles_per_seq, tile % tiles_per_seq

    def this_tile(t):
        return split(jnp.minimum(t, n_tiles - 1))

    def previous_tile(t):
        return split(jnp.maximum(t - 1, 0))

    return this_tile, previous_tile


def _layer_a(x, w_in, w_group, scale, w_out, ln_g, ln_b, w_kv, cos, sin_signed,
             next_w_qg, next_w_out):
    batch, seq, _ = x.shape
    tm = TILE_A
    tiles_per_seq = seq // tm
    n_tiles = batch * tiles_per_seq
    this_tile, previous_tile = _tile_maps(n_tiles, tiles_per_seq)
    lagged = lambda width: pl.BlockSpec((1, tm, width), lambda t: (*previous_tile(t), 0))
    lagged_table = pl.BlockSpec((tm, LANES), lambda t: (previous_tile(t)[1], 0))
    cast_rows = D_MODEL // n_tiles
    assert cast_rows * n_tiles == D_MODEL and cast_rows % BF16_SUBLANES == 0
    cast_slice = lambda width: pl.BlockSpec(
        (cast_rows, width), lambda t: (jnp.minimum(t, n_tiles - 1), 0))
    return pl.pallas_call(
        functools.partial(_layer_a_kernel, n_tiles=n_tiles, tiles_per_seq=tiles_per_seq),
        grid=(n_tiles + 1,),
        in_specs=[
            pl.BlockSpec((1, tm, D_MODEL), lambda t: (*this_tile(t), 0)),
            _resident((D_MODEL, 2 * D_MODEL)),
            _resident((N_POOL_GROUPS, POOL_GROUP_DIM, POOL_GROUP_DIM)),
            _resident((1, D_MODEL)),
            _resident((D_MODEL, D_MODEL)),
            _resident((1, D_MODEL)),
            _resident((1, D_MODEL)),
            _resident((D_MODEL, 2 * KV_WIDTH)),
            lagged_table, lagged_table,
            cast_slice(2 * D_MODEL), cast_slice(D_MODEL),
        ],
        out_specs=[lagged(D_MODEL), lagged(KV_WIDTH), lagged(KV_WIDTH),
                   cast_slice(2 * D_MODEL), cast_slice(D_MODEL)],
        out_shape=[
            jax.ShapeDtypeStruct((batch, seq, D_MODEL), F32),
            jax.ShapeDtypeStruct((batch, seq, KV_WIDTH), BF16),
            jax.ShapeDtypeStruct((batch, seq, KV_WIDTH), BF16),
            jax.ShapeDtypeStruct((D_MODEL, 2 * D_MODEL), BF16),
            jax.ShapeDtypeStruct((D_MODEL, D_MODEL), BF16),
        ],
        scratch_shapes=[
            pltpu.VMEM((MAX_WINDOW, D_MODEL), F32),
            pltpu.VMEM((tm, D_MODEL), BF16),
            pltpu.VMEM((tm, D_MODEL), F32),
        ],
        compiler_params=pltpu.CompilerParams(
            dimension_semantics=("arbitrary",),
            vmem_limit_bytes=VMEM_LIMIT_BYTES),
        name="yoco_pool_layer",
    )(x, w_in, w_group, scale, w_out, ln_g, ln_b, w_kv, cos, sin_signed, next_w_qg, next_w_out)


def _layer_b(x1, k, v, w_qg, sinks, w_out, ln_g, ln_b, cos, sin_signed):
    batch, seq, _ = x1.shape
    tq = TILE_B
    blocks_per_tile = tq // ATTN_BLOCK
    tiles_per_seq = seq // tq
    n_tiles = batch * tiles_per_seq
    this_tile, previous_tile = _tile_maps(n_tiles, tiles_per_seq)
    tok = lambda width: pl.BlockSpec((1, tq, width), lambda t: (*this_tile(t), 0))

    def previous_block(t):
        b, i = this_tile(t)
        return b, jnp.maximum(i * blocks_per_tile - 1, 0), 0

    prev = pl.BlockSpec((1, ATTN_BLOCK, KV_WIDTH), previous_block)
    table = pl.BlockSpec((tq, LANES), lambda t: (this_tile(t)[1], 0))
    padded = N_KV_HEADS * LANES
    return pl.pallas_call(
        functools.partial(_layer_b_kernel, n_tiles=n_tiles, tiles_per_seq=tiles_per_seq),
        grid=(n_tiles + 1,),
        in_specs=[
            pl.BlockSpec(memory_space=pltpu.SMEM),
            tok(D_MODEL),
            tok(KV_WIDTH), prev, tok(KV_WIDTH), prev,
            _resident((D_MODEL, 2 * D_MODEL)),
            _resident((D_MODEL, D_MODEL)),
            _resident((1, D_MODEL)),
            _resident((1, D_MODEL)),
            table, table,
        ],
        out_specs=pl.BlockSpec((1, tq, D_MODEL), lambda t: (*previous_tile(t), 0)),
        out_shape=jax.ShapeDtypeStruct((batch, seq, D_MODEL), F32),
        scratch_shapes=[
            pltpu.VMEM((tq, D_MODEL), BF16),
            pltpu.VMEM((tq + ATTN_BLOCK, padded), BF16),
            pltpu.VMEM((tq + ATTN_BLOCK, padded), BF16),
            pltpu.VMEM((tq + ATTN_BLOCK, padded), BF16),
            pltpu.VMEM((tq + ATTN_BLOCK, padded), BF16),
            pltpu.VMEM((blocks_per_tile * N_KV_HEADS,
                        PAIRS_PER_KV * ATTN_BLOCK, HEADS_PER_VREG * 2 * ATTN_BLOCK), BF16),
            pltpu.VMEM((tq, D_MODEL), F32),
            pltpu.VMEM((tq, D_MODEL), F32),
            pltpu.VMEM((tq, D_MODEL), F32),
        ],
        compiler_params=pltpu.CompilerParams(
            dimension_semantics=("arbitrary",),
            vmem_limit_bytes=VMEM_LIMIT_BYTES),
        name="yoco_swa_layer",
    )(sinks, x1, k, k, v, v, w_qg, w_out, ln_g, ln_b, cos, sin_signed)


def kernel(x, ln_g, ln_b, a_w_in, a_w_group, a_scale, a_w_out, b_w_k, b_w_v, b_w_qg, b_sinks, b_w_out):
    assert a_w_in.shape[0] == 1 and b_w_qg.shape[0] == 1
    seq = x.shape[1]
    cos, sin_signed = _rope_tables(seq)
    w_kv = jnp.concatenate([b_w_k, b_w_v], axis=1).astype(BF16)
    x1, k, v, w_qg_bf, b_w_out_bf = _layer_a(
        x, a_w_in[0].astype(BF16), a_w_group[0].astype(BF16), a_scale[0][None, :],
        a_w_out[0].astype(BF16), ln_g[0][None, :], ln_b[0][None, :], w_kv, cos, sin_signed,
        b_w_qg[0], b_w_out[0])
    return _layer_b(
        x1, k, v, w_qg_bf, b_sinks[0], b_w_out_bf,
        ln_g[1][None, :], ln_b[1][None, :], cos, sin_signed)
```

```python
import functools

import jax
import jax.numpy as jnp
from jax import lax
from jax.experimental import pallas as pl
from jax.experimental.pallas import tpu as pltpu

D_MODEL = 2048
DEPTH = 2
POOL_WINDOWS = (2, 4, 8, 16)
N_POOL_GROUPS = len(POOL_WINDOWS)
POOL_GROUP_DIM = D_MODEL // N_POOL_GROUPS
MAX_WINDOW = max(POOL_WINDOWS)
HEAD_DIM = 64
HALF_DIM = HEAD_DIM // 2
N_Q_HEADS = D_MODEL // HEAD_DIM
N_KV_HEADS = N_Q_HEADS // 8
GQA_GROUP = N_Q_HEADS // N_KV_HEADS
KV_WIDTH = N_KV_HEADS * HEAD_DIM
ATTN_BLOCK = 128
ROPE_THETA = 10000.0
LN_EPS = 1e-5
NEG_INF = -1e30
LOG2_E = 1.4426950408889634
DEEPNORM_ALPHA = (2 * DEPTH) ** 0.25

LANES = 128
SUBLANES = 8
BF16_SUBLANES = 16
QUERY_SUB = 64
KEY_WIN = QUERY_SUB + ATTN_BLOCK
SCORE_LOOKAHEAD = 3
HEADS_PER_VREG = LANES // HEAD_DIM
PAIRS_PER_KV = GQA_GROUP // HEADS_PER_VREG
VMEM_LIMIT_BYTES = 56 * 1024 * 1024
TILE_A = 256
TILE_B = 256

F32 = jnp.float32
BF16 = jnp.bfloat16


def _dot(a, b):
    return jnp.dot(a, b, preferred_element_type=F32)


def _silu(z):
    return z / (1.0 + jnp.exp(-z))


def _layer_norm(r, g, b):
    mu = jnp.mean(r, axis=-1, keepdims=True)
    c = r - mu
    var = jnp.mean(c * c, axis=-1, keepdims=True)
    return c * lax.rsqrt(var + LN_EPS) * g + b


def _rope(t, cos, sin_signed):
    width = t.shape[1]
    reps = width // LANES
    cos_w = jnp.concatenate([cos] * reps, axis=1)
    sin_w = jnp.concatenate([sin_signed] * reps, axis=1)
    lane = lax.broadcasted_iota(jnp.int32, (1, width), 1)
    first_half = (lane % HEAD_DIM) < HALF_DIM
    partner = jnp.where(first_half,
                        pltpu.roll(t, width - HALF_DIM, axis=1),
                        pltpu.roll(t, HALF_DIM, axis=1))
    return t * cos_w + partner * sin_w


def _layer_a_kernel(x_ref, win_ref, wg_ref, scale_ref, wout_ref, g_ref, b_ref, wkv_ref,
                    cos_ref, sin_ref, next_wqg_ref, next_wout_ref,
                    x1_ref, k_ref, v_ref, next_wqg_bf_ref, next_wout_bf_ref,
                    carry_ref, y_ref, r_ref, *, n_tiles, tiles_per_seq):
    tm = x_ref.shape[1]
    t = pl.program_id(0)

    next_wqg_bf_ref[...] = next_wqg_ref[...].astype(BF16)
    next_wout_bf_ref[...] = next_wout_ref[...].astype(BF16)

    @pl.when(t == 0)
    def _():
        r_ref[...] = jnp.zeros_like(r_ref)
        carry_ref[...] = jnp.zeros_like(carry_ref)

    def finish_previous_tile():
        x1 = _layer_norm(r_ref[...], g_ref[...], b_ref[...])
        x1_ref[0] = x1
        kv = _dot(x1.astype(BF16), wkv_ref[...])
        k = _rope(kv[:, :KV_WIDTH], cos_ref[...], sin_ref[...])
        k_ref[0] = k.astype(BF16)
        v_ref[0] = kv[:, KV_WIDTH:].astype(BF16)

    def mix_this_tile():
        i = t % tiles_per_seq
        x = x_ref[0]
        xb = x.astype(BF16)
        pos1 = lax.broadcasted_iota(jnp.int32, (tm, 1), 0) + (i * tm + 1)
        for g, w in enumerate(POOL_WINDOWS):
            lo, hi = g * POOL_GROUP_DIM, (g + 1) * POOL_GROUP_DIM
            u = _dot(xb, win_ref[:, lo:hi])
            z = _dot(xb, win_ref[:, D_MODEL + lo:D_MODEL + hi])
            history = jnp.where(i > 0, carry_ref[:, lo:hi], 0.0)
            acc = jnp.concatenate([history, u], axis=0)
            carry_ref[:, lo:hi] = u[tm - MAX_WINDOW:, :]
            shift = 1
            while shift < w:
                acc = acc + pltpu.roll(acc, shift, axis=0)
                shift *= 2
            inv_count = 1.0 / jnp.minimum(pos1, w).astype(F32)
            pooled = acc[MAX_WINDOW:, :] * inv_count - u
            mixed = _dot(pooled.astype(BF16), wg_ref[g])
            y = mixed * scale_ref[:, lo:hi] * _silu(z)
            y_ref[:, lo:hi] = y.astype(BF16)
        r_ref[...] = DEEPNORM_ALPHA * x + _dot(y_ref[...], wout_ref[...])

    @pl.when(t < n_tiles)
    def _():
        finish_previous_tile()
        mix_this_tile()

    @pl.when(t == n_tiles)
    def _():
        finish_previous_tile()


def _pad_heads(t):
    lane = lax.broadcasted_iota(jnp.int32, (1, LANES), 1)
    low = lane < HEAD_DIM
    lo_cols, hi_cols = [], []
    for j in range(KV_WIDTH // LANES):
        col = t[:, j * LANES:(j + 1) * LANES]
        swapped = pltpu.roll(col, HEAD_DIM, axis=1)
        lo_cols += [jnp.where(low, col, 0.0), jnp.where(low, swapped, 0.0)]
        hi_cols += [jnp.where(low, 0.0, swapped), jnp.where(low, 0.0, col)]
    return (jnp.concatenate(lo_cols, axis=1).astype(BF16),
            jnp.concatenate(hi_cols, axis=1).astype(BF16))


def _layer_b_kernel(sinks_ref, x1_ref, kc_ref, kp_ref, vc_ref, vp_ref, wqg_ref, wout_ref,
                    g_ref, b_ref, cos_ref, sin_ref, out_ref,
                    q_ref, k0_ref, k1_ref, vt_ref, o_ref, gate_ref, r_ref,
                    *, n_tiles, tiles_per_seq):
    tq = x1_ref.shape[1]
    t = pl.program_id(0)

    @pl.when(t == 0)
    def _():
        r_ref[...] = jnp.zeros_like(r_ref)

    def finish_previous_tile():
        out_ref[0] = _layer_norm(r_ref[...], g_ref[...], b_ref[...])

    def attend_this_tile():
        i = t % tiles_per_seq
        x1 = x1_ref[0]
        xb = x1.astype(BF16)

        q = _rope(_dot(xb, wqg_ref[:, :D_MODEL]), cos_ref[...], sin_ref[...])
        q_ref[...] = (q * (HEAD_DIM ** -0.5 * LOG2_E)).astype(BF16)

        kf = jnp.concatenate([kp_ref[0], kc_ref[0]], axis=0).astype(F32)
        vf = jnp.concatenate([vp_ref[0], vc_ref[0]], axis=0).astype(F32)
        k0_ref[...], k1_ref[...] = _pad_heads(kf)
        vt_ref[...] = jnp.transpose(vf).astype(BF16)

        gate_ref[...] = _silu(_dot(xb, wqg_ref[:, D_MODEL:]))

        key = lax.broadcasted_iota(jnp.int32, (KEY_WIN, 2 * ATTN_BLOCK), 0)
        qry = lax.broadcasted_iota(jnp.int32, (KEY_WIN, 2 * ATTN_BLOCK), 1) % QUERY_SUB
        in_window = (key > qry) & (key <= qry + ATTN_BLOCK)
        lane = lax.broadcasted_iota(jnp.int32, (1, 2 * ATTN_BLOCK), 1)
        key_top = lax.broadcasted_iota(jnp.int32, (SUBLANES, 2 * ATTN_BLOCK), 0)
        zero_keys = jnp.zeros((QUERY_SUB, 2 * ATTN_BLOCK), BF16)
        zero_half = jnp.zeros((HEAD_DIM, 2 * ATTN_BLOCK), BF16)
        ones_rows = jnp.ones((BF16_SUBLANES, 2 * ATTN_BLOCK), BF16)
        zero_rows = jnp.zeros((BF16_SUBLANES, 2 * ATTN_BLOCK), BF16)

        def scores(n, g, sub):
            cols = slice(g * LANES, (g + 1) * LANES)
            pair0 = g * PAIRS_PER_KV
            first = n * ATTN_BLOCK + sub * QUERY_SUB
            queries = pl.ds(first, QUERY_SUB)
            window = pl.ds(first, KEY_WIN)
            q_sub = jnp.concatenate(
                [q_ref[queries, (pair0 + p) * LANES:(pair0 + p + 1) * LANES]
                 for p in range(PAIRS_PER_KV)], axis=0)
            k_win = jnp.concatenate([k0_ref[window, cols], k1_ref[window, cols]], axis=0)
            return lax.dot_general(k_win, q_sub, (((1,), (1,)), ((), ())),
                                   preferred_element_type=F32)

        def finish(n, g, sub, s_t):
            pair0 = g * PAIRS_PER_KV
            band = slice(n * ATTN_BLOCK, (n + 2) * ATTN_BLOCK)
            vt = vt_ref[g * HEAD_DIM:(g + 1) * HEAD_DIM, band]
            queries = pl.ds(n * ATTN_BLOCK + sub * QUERY_SUB, QUERY_SUB)
            valid = in_window
            if n == 0:
                valid = valid & ((i > 0) | (key >= ATTN_BLOCK - sub * QUERY_SUB))
            if True:
                probs = []
                for e in range(HEADS_PER_VREG):
                    s = s_t[e * KEY_WIN:(e + 1) * KEY_WIN]
                    sink = jnp.full((1, 2 * ATTN_BLOCK), NEG_INF, F32)
                    for p in range(PAIRS_PER_KV):
                        head = (pair0 + p) * HEADS_PER_VREG + e
                        sink = jnp.where(lane // QUERY_SUB == p, sinks_ref[head] * LOG2_E, sink)
                    top = jnp.where(valid[:SUBLANES], s[:SUBLANES],
                                    jnp.where(key_top == 0, sink, NEG_INF))
                    rest = jnp.where(valid[SUBLANES:], s[SUBLANES:], NEG_INF)
                    s = jnp.concatenate([top, rest], axis=0)
                    m = jnp.max(s, axis=0, keepdims=True)
                    p_t = jnp.exp2(s - m).astype(BF16)
                    probs += [p_t, zero_keys] if sub == 0 else [zero_keys, p_t]
                p_all = jnp.concatenate(probs, axis=0)
                keep = jnp.where(lane == sub * QUERY_SUB, 0.0, 1.0).astype(BF16)
                vt_keep = vt * keep
                v_aug = jnp.concatenate(
                    [jnp.concatenate([vt_keep, zero_half], axis=1),
                     jnp.concatenate([zero_half, vt_keep], axis=1),
                     jnp.concatenate([ones_rows, zero_rows], axis=1),
                     jnp.concatenate([zero_rows, ones_rows], axis=1)], axis=0)
                o_t = _dot(v_aug, p_all)
                inv0 = 1.0 / o_t[2 * HEAD_DIM:2 * HEAD_DIM + 1]
                inv1 = 1.0 / o_t[2 * HEAD_DIM + BF16_SUBLANES:2 * HEAD_DIM + BF16_SUBLANES + 1]
                out_t = jnp.concatenate([o_t[:HEAD_DIM] * inv0,
                                         o_t[HEAD_DIM:2 * HEAD_DIM] * inv1], axis=0)
                out = jnp.transpose(out_t)
                for p in range(PAIRS_PER_KV):
                    o_ref[queries, (pair0 + p) * LANES:(pair0 + p + 1) * LANES] = (
                        out[p * QUERY_SUB:(p + 1) * QUERY_SUB])

        @pl.when(i >= 0)
        def _():
            work = [(n, g, sub) for n in range(tq // ATTN_BLOCK) for g in range(N_KV_HEADS)
                    for sub in range(ATTN_BLOCK // QUERY_SUB)]
            pending = []
            for step in range(len(work) + SCORE_LOOKAHEAD):
                if step < len(work):
                    pending.append((work[step], scores(*work[step])))
                if step >= SCORE_LOOKAHEAD:
                    item, s_t = pending.pop(0)
                    finish(*item, s_t)

        finish_previous_tile()
        gated = (o_ref[...] * gate_ref[...]).astype(BF16)
        r_ref[...] =DEEPNORM_ALPHA * x1 + _dot(gated, wout_ref[...])

    @pl.when(t < n_tiles)
    def _():
        attend_this_tile()

    @pl.when(t == n_tiles)
    def _():
        finish_previous_tile()


def _resident(shape):
    return pl.BlockSpec(shape, lambda *_: (0,) * len(shape), pipeline_mode=pl.Buffered(1))


def _rope_tables(seq):
    lane = jnp.arange(LANES)
    inv_freq = ROPE_THETA ** (-(2 * (lane % HALF_DIM)).astype(F32) / HEAD_DIM)
    ang = jnp.arange(seq, dtype=F32)[:, None] * inv_freq[None, :]
    sign = jnp.where((lane % HEAD_DIM) < HALF_DIM, -1.0, 1.0).astype(F32)
    return jnp.cos(ang), jnp.sin(ang) * sign[None, :]


def _tile_maps(n_tiles, tiles_per_seq):
    def split(tile):
        return tile // tiles_per_seq, tile % tiles_per_seq

    def this_tile(t):
        return split(jnp.minimum(t, n_tiles - 1))

    def previous_tile(t):
        return split(jnp.maximum(t - 1, 0))

    return this_tile, previous_tile


def _layer_a(x, w_in, w_group, scale, w_out, ln_g, ln_b, w_kv, cos, sin_signed,
             next_w_qg, next_w_out):
    batch, seq, _ = x.shape
    tm = TILE_A
    tiles_per_seq = seq // tm
    n_tiles = batch * tiles_per_seq
    this_tile, previous_tile = _tile_maps(n_tiles, tiles_per_seq)
    lagged = lambda width: pl.BlockSpec((1, tm, width), lambda t: (*previous_tile(t), 0))
    lagged_table = pl.BlockSpec((tm, LANES), lambda t: (previous_tile(t)[1], 0))
    cast_rows = D_MODEL // n_tiles
    assert cast_rows * n_tiles == D_MODEL and cast_rows % BF16_SUBLANES == 0
    cast_slice = lambda width: pl.BlockSpec(
        (cast_rows, width), lambda t: (jnp.minimum(t, n_tiles - 1), 0))
    return pl.pallas_call(
        functools.partial(_layer_a_kernel, n_tiles=n_tiles, tiles_per_seq=tiles_per_seq),
        grid=(n_tiles + 1,),
        in_specs=[
            pl.BlockSpec((1, tm, D_MODEL), lambda t: (*this_tile(t), 0)),
            _resident((D_MODEL, 2 * D_MODEL)),
            _resident((N_POOL_GROUPS, POOL_GROUP_DIM, POOL_GROUP_DIM)),
            _resident((1, D_MODEL)),
            _resident((D_MODEL, D_MODEL)),
            _resident((1, D_MODEL)),
            _resident((1, D_MODEL)),
            _resident((D_MODEL, 2 * KV_WIDTH)),
            lagged_table, lagged_table,
            cast_slice(2 * D_MODEL), cast_slice(D_MODEL),
        ],
        out_specs=[lagged(D_MODEL), lagged(KV_WIDTH), lagged(KV_WIDTH),
                   cast_slice(2 * D_MODEL), cast_slice(D_MODEL)],
        out_shape=[
            jax.ShapeDtypeStruct((batch, seq, D_MODEL), F32),
            jax.ShapeDtypeStruct((batch, seq, KV_WIDTH), BF16),
            jax.ShapeDtypeStruct((batch, seq, KV_WIDTH), BF16),
            jax.ShapeDtypeStruct((D_MODEL, 2 * D_MODEL), BF16),
            jax.ShapeDtypeStruct((D_MODEL, D_MODEL), BF16),
        ],
        scratch_shapes=[
            pltpu.VMEM((MAX_WINDOW, D_MODEL), F32),
            pltpu.VMEM((tm, D_MODEL), BF16),
            pltpu.VMEM((tm, D_MODEL), F32),
        ],
        compiler_params=pltpu.CompilerParams(
            dimension_semantics=("arbitrary",),
            vmem_limit_bytes=VMEM_LIMIT_BYTES),
        name="yoco_pool_layer",
    )(x, w_in, w_group, scale, w_out, ln_g, ln_b, w_kv, cos, sin_signed, next_w_qg, next_w_out)


def _layer_b(x1, k, v, w_qg, sinks, w_out, ln_g, ln_b, cos, sin_signed):
    batch, seq, _ = x1.shape
    tq = TILE_B
    blocks_per_tile = tq // ATTN_BLOCK
    tiles_per_seq = seq // tq
    n_tiles = batch * tiles_per_seq
    this_tile, previous_tile = _tile_maps(n_tiles, tiles_per_seq)
    tok = lambda width: pl.BlockSpec((1, tq, width), lambda t: (*this_tile(t), 0))

    def previous_block(t):
        b, i = this_tile(t)
        return b, jnp.maximum(i * blocks_per_tile - 1, 0), 0

    prev = pl.BlockSpec((1, ATTN_BLOCK, KV_WIDTH), previous_block)
    table = pl.BlockSpec((tq, LANES), lambda t: (this_tile(t)[1], 0))
    padded = N_KV_HEADS * LANES
    return pl.pallas_call(
        functools.partial(_layer_b_kernel, n_tiles=n_tiles, tiles_per_seq=tiles_per_seq),
        grid=(n_tiles + 1,),
        in_specs=[
            pl.BlockSpec(memory_space=pltpu.SMEM),
            tok(D_MODEL),
            tok(KV_WIDTH), prev, tok(KV_WIDTH), prev,
            _resident((D_MODEL, 2 * D_MODEL)),
            _resident((D_MODEL, D_MODEL)),
            _resident((1, D_MODEL)),
            _resident((1, D_MODEL)),
            table, table,
        ],
        out_specs=pl.BlockSpec((1, tq, D_MODEL), lambda t: (*previous_tile(t), 0)),
        out_shape=jax.ShapeDtypeStruct((batch, seq, D_MODEL), F32),
        scratch_shapes=[
            pltpu.VMEM((tq, D_MODEL), BF16),
            pltpu.VMEM((tq + ATTN_BLOCK, padded), BF16),
            pltpu.VMEM((tq + ATTN_BLOCK, padded), BF16),
            pltpu.VMEM((KV_WIDTH, tq + ATTN_BLOCK), BF16),
            pltpu.VMEM((tq, D_MODEL), F32),
            pltpu.VMEM((tq, D_MODEL), F32),
            pltpu.VMEM((tq, D_MODEL), F32),
        ],
        compiler_params=pltpu.CompilerParams(
            dimension_semantics=("arbitrary",),
            vmem_limit_bytes=VMEM_LIMIT_BYTES),
        name="yoco_swa_layer",
    )(sinks, x1, k, k, v, v, w_qg, w_out, ln_g, ln_b, cos, sin_signed)


def kernel(x, ln_g, ln_b, a_w_in, a_w_group, a_scale, a_w_out, b_w_k, b_w_v, b_w_qg, b_sinks, b_w_out):
    assert a_w_in.shape[0] == 1 and b_w_qg.shape[0] == 1
    seq = x.shape[1]
    cos, sin_signed = _rope_tables(seq)
    w_kv = jnp.concatenate([b_w_k, b_w_v], axis=1).astype(BF16)
    x1, k, v, w_qg_bf, b_w_out_bf = _layer_a(
        x, a_w_in[0].astype(BF16), a_w_group[0].astype(BF16), a_scale[0][None, :],
        a_w_out[0].astype(BF16), ln_g[0][None, :], ln_b[0][None, :], w_kv, cos, sin_signed,
        b_w_qg[0], b_w_out[0])
    return _layer_b(
        x1, k, v, w_qg_bf, b_sinks[0], b_w_out_bf,
        ln_g[1][None, :], ln_b[1][None, :], cos, sin_signed)
```

```python
import functools

import jax
import jax.numpy as jnp
from jax import lax
from jax.experimental import pallas as pl
from jax.experimental.pallas import tpu as pltpu

D_MODEL = 2048
DEPTH = 2
POOL_WINDOWS = (2, 4, 8, 16)
N_POOL_GROUPS = len(POOL_WINDOWS)
POOL_GROUP_DIM = D_MODEL // N_POOL_GROUPS
MAX_WINDOW = max(POOL_WINDOWS)
HEAD_DIM = 64
HALF_DIM = HEAD_DIM // 2
N_Q_HEADS = D_MODEL // HEAD_DIM
N_KV_HEADS = N_Q_HEADS // 8
GQA_GROUP = N_Q_HEADS // N_KV_HEADS
KV_WIDTH = N_KV_HEADS * HEAD_DIM
ATTN_BLOCK = 128
ROPE_THETA = 10000.0
LN_EPS = 1e-5
NEG_INF = -1e30
LOG2_E = 1.4426950408889634
DEEPNORM_ALPHA = (2 * DEPTH) ** 0.25

LANES = 128
SUBLANES = 8
BF16_SUBLANES = 16
QUERY_SUB = 64
KEY_WIN = QUERY_SUB + ATTN_BLOCK
SCORE_LOOKAHEAD = 3
HEADS_PER_VREG = LANES // HEAD_DIM
PAIRS_PER_KV = GQA_GROUP // HEADS_PER_VREG
VMEM_LIMIT_BYTES = 56 * 1024 * 1024
TILE_A = 256
TILE_B = 256
CAST_ROWS = 512
CAST_COLS = 512

F32 = jnp.float32
BF16 = jnp.bfloat16


def _dot(a, b):
    return jnp.dot(a, b, preferred_element_type=F32)


def _silu(z):
    half = 0.5 * z
    return half + half * jnp.tanh(half)


def _layer_norm(r, g, b):
    mu = jnp.mean(r, axis=-1, keepdims=True)
    c = r - mu
    var = jnp.mean(c * c, axis=-1, keepdims=True)
    return c * lax.rsqrt(var + LN_EPS) * g + b


def _tile_rope_tables(i, cos_in_ref, sin_in_ref, cos_base_ref, sin_base_ref):
    cb, sb = cos_base_ref[pl.ds(i, 1), :], sin_base_ref[pl.ds(i, 1), :]
    ci, si = cos_in_ref[...], sin_in_ref[...]
    lane = lax.broadcasted_iota(jnp.int32, (1, LANES), 1)
    sign = jnp.where((lane % HEAD_DIM) < HALF_DIM, -1.0, 1.0)
    return cb * ci - sb * si, (sb * ci + cb * si) * sign


def _rope(t, cos, sin_signed):
    width = t.shape[1]
    reps = width // LANES
    cos_w = jnp.concatenate([cos] * reps, axis=1)
    sin_w = jnp.concatenate([sin_signed] * reps, axis=1)
    lane = lax.broadcasted_iota(jnp.int32, (1, width), 1)
    first_half = (lane % HEAD_DIM) < HALF_DIM
    partner = jnp.where(first_half,
                        pltpu.roll(t, width - HALF_DIM, axis=1),
                        pltpu.roll(t, HALF_DIM, axis=1))
    return t * cos_w + partner * sin_w


def _cast_jobs(win_hbm, wg_hbm, wout_hbm, wk_hbm, wv_hbm,
               wmix_ref, wz_ref, wg_ref, wout_ref, wkv_ref):
    def fold(g):
        return lambda chunk: _dot(chunk, wg_ref[g]).astype(BF16)

    jobs = [(wg_hbm.at[g], wg_ref.at[g], None) for g in range(N_POOL_GROUPS)]
    for r in range(0, D_MODEL, CAST_ROWS):
        for g in range(N_POOL_GROUPS):
            rows, cols = pl.ds(r, CAST_ROWS), pl.ds(g * POOL_GROUP_DIM, POOL_GROUP_DIM)
            jobs.append((win_hbm.at[rows, cols], wmix_ref.at[rows, cols], fold(g)))
    for src, col0, dst in ((win_hbm, D_MODEL, wz_ref), (wout_hbm, 0, wout_ref)):
        for r in range(0, D_MODEL, CAST_ROWS):
            for c in range(0, D_MODEL, CAST_COLS):
                rows = pl.ds(r, CAST_ROWS)
                jobs.append((src.at[rows, pl.ds(col0 + c, CAST_COLS)],
                             dst.at[rows, pl.ds(c, CAST_COLS)], None))
    for j, src in enumerate((wk_hbm, wv_hbm)):
        for r in range(0, D_MODEL, CAST_ROWS):
            jobs.append((src.at[pl.ds(r, CAST_ROWS), :],
                         wkv_ref.at[pl.ds(r, CAST_ROWS), pl.ds(j * KV_WIDTH, KV_WIDTH)], None))
    return jobs


def _load_and_cast_weights(jobs, stage_ref, sem_ref):
    def copy(k):
        src, dst, _ = jobs[k]
        rows, cols = dst.shape
        slot = k % 2
        return pltpu.make_async_copy(
            src, stage_ref.at[slot, pl.ds(0, rows), pl.ds(0, cols)], sem_ref.at[slot])

    copy(0).start()
    for k, (_, dst, transform) in enumerate(jobs):
        if k + 1 < len(jobs):
            copy(k + 1).start()
        copy(k).wait()
        rows, cols = dst.shape
        chunk = stage_ref[k % 2, pl.ds(0, rows), pl.ds(0, cols)].astype(BF16)
        dst[...] = chunk if transform is None else transform(chunk)


def _layer_a_kernel(x_ref, win_hbm, wg_hbm, scale_ref, wout_hbm, g_ref, b_ref, wk_hbm, wv_hbm,
                    cos_in_ref, sin_in_ref, cos_base_ref, sin_base_ref,
                    next_wqg_ref, next_wout_ref,
                    x1_ref, k_ref, v_ref, next_wqg_bf_ref, next_wout_bf_ref,
                    wmix_ref, wz_ref, wg_ref, wout_ref, wkv_ref, stage_ref, sem_ref,
                    pooled_ref, gate_ref, carry_ref, y_ref, r_ref, *, n_tiles, tiles_per_seq):
    tm = x_ref.shape[1]
    t = pl.program_id(0)

    @pl.when(t == 0)
    def _():
        _load_and_cast_weights(
            _cast_jobs(win_hbm, wg_hbm, wout_hbm, wk_hbm, wv_hbm,
                       wmix_ref, wz_ref, wg_ref, wout_ref, wkv_ref), stage_ref, sem_ref)
        r_ref[...] = jnp.zeros_like(r_ref)
        carry_ref[...] = jnp.zeros_like(carry_ref)

    def normalise_previous_tile():
        next_wqg_bf_ref[...] = next_wqg_ref[...].astype(BF16)
        next_wout_bf_ref[...] = next_wout_ref[...].astype(BF16)
        x1 = _layer_norm(r_ref[...], g_ref[...], b_ref[...])
        x1_ref[0] = x1
        return x1.astype(BF16)

    def project_kv_previous_tile(x1_bf):
        kv = _dot(x1_bf, wkv_ref[...])
        i_prev = jnp.maximum(t - 1, 0) % tiles_per_seq
        k = _rope(kv[:, :KV_WIDTH], *_tile_rope_tables(
            i_prev, cos_in_ref, sin_in_ref, cos_base_ref, sin_base_ref))
        k_ref[0] = k.astype(BF16)
        v_ref[0] = kv[:, KV_WIDTH:].astype(BF16)

    def mix_this_tile():
        i = t % tiles_per_seq
        x = x_ref[0]
        xb = x.astype(BF16)
        pos1 = lax.broadcasted_iota(jnp.int32, (tm, 1), 0) + (i * tm + 1)
        gate_ref[...] = _silu(_dot(xb, wz_ref[...])) * scale_ref[...]
        for c in range(0, D_MODEL, POOL_GROUP_DIM):
            cols = slice(c, c + POOL_GROUP_DIM)
            history = jnp.where(i > 0, carry_ref[:, cols], 0.0)
            acc = jnp.concatenate([history, x[:, cols]], axis=0)
            shift = 1
            for g, w in enumerate(POOL_WINDOWS):
                while shift < w:
                    acc = acc + pltpu.roll(acc, shift, axis=0)
                    shift *= 2
                inv_count = 1.0 / jnp.minimum(pos1, w).astype(F32)
                pooled_ref[g, :, cols] = (acc[MAX_WINDOW:, :] * inv_count - x[:, cols]).astype(BF16)
        carry_ref[...] = x[tm - MAX_WINDOW:, :]
        for g in range(N_POOL_GROUPS):
            lo, hi = g * POOL_GROUP_DIM, (g + 1) * POOL_GROUP_DIM
            mixed = _dot(pooled_ref[g], wmix_ref[:, lo:hi])
            y_ref[:, lo:hi] = (mixed * gate_ref[:, lo:hi]).astype(BF16)
        r_ref[...] = DEEPNORM_ALPHA * x + _dot(y_ref[...], wout_ref[...])

    @pl.when(t < n_tiles)
    def _():
        x1_bf = normalise_previous_tile()
        mix_this_tile()
        project_kv_previous_tile(x1_bf)

    @pl.when(t == n_tiles)
    def _():
        project_kv_previous_tile(normalise_previous_tile())


def _pad_heads(t):
    lane = lax.broadcasted_iota(jnp.int32, (1, LANES), 1)
    low = lane < HEAD_DIM
    lo_cols, hi_cols = [], []
    for j in range(KV_WIDTH // LANES):
        col = t[:, j * LANES:(j + 1) * LANES]
        swapped = pltpu.roll(col, HEAD_DIM, axis=1)
        lo_cols += [jnp.where(low, col, 0.0), jnp.where(low, swapped, 0.0)]
        hi_cols += [jnp.where(low, 0.0, swapped), jnp.where(low, 0.0, col)]
    return (jnp.concatenate(lo_cols, axis=1).astype(BF16),
            jnp.concatenate(hi_cols, axis=1).astype(BF16))


def _layer_b_kernel(sinks_ref, x1_ref, kc_ref, kp_ref, vc_ref, vp_ref, wqg_ref, wout_ref,
                    g_ref, b_ref, cos_in_ref, sin_in_ref, cos_base_ref, sin_base_ref, out_ref,
                    q_ref, k0_ref, k1_ref, vt_ref, o_ref, gate_ref, r_ref,
                    *, n_tiles, tiles_per_seq):
    tq = x1_ref.shape[1]
    t = pl.program_id(0)

    @pl.when(t == 0)
    def _():
        r_ref[...] = jnp.zeros_like(r_ref)

    def finish_previous_tile(rows=slice(None)):
        out_ref[0, rows, :] = _layer_norm(r_ref[rows, :], g_ref[...], b_ref[...])

    def attend_this_tile():
        i = t % tiles_per_seq
        x1 = x1_ref[0]
        xb = x1.astype(BF16)

        q = _rope(_dot(xb, wqg_ref[:, :D_MODEL]), *_tile_rope_tables(
            i, cos_in_ref, sin_in_ref, cos_base_ref, sin_base_ref))
        q_ref[...] = (q * (HEAD_DIM ** -0.5 * LOG2_E)).astype(BF16)

        kf = jnp.concatenate([kp_ref[0], kc_ref[0]], axis=0).astype(F32)
        vf = jnp.concatenate([vp_ref[0], vc_ref[0]], axis=0).astype(F32)
        k0_ref[...], k1_ref[...] = _pad_heads(kf)
        vt_ref[...] = jnp.transpose(vf).astype(BF16)

        gate_ref[...] = _silu(_dot(xb, wqg_ref[:, D_MODEL:]))

        key = lax.broadcasted_iota(jnp.int32, (KEY_WIN, 2 * ATTN_BLOCK), 0)
        qry = lax.broadcasted_iota(jnp.int32, (KEY_WIN, 2 * ATTN_BLOCK), 1) % QUERY_SUB
        in_window = (key > qry) & (key <= qry + ATTN_BLOCK)
        lane = lax.broadcasted_iota(jnp.int32, (1, 2 * ATTN_BLOCK), 1)
        key_top = lax.broadcasted_iota(jnp.int32, (SUBLANES, 2 * ATTN_BLOCK), 0)
        zero_keys = jnp.zeros((QUERY_SUB, 2 * ATTN_BLOCK), BF16)
        zero_half = jnp.zeros((HEAD_DIM, 2 * ATTN_BLOCK), BF16)
        slot_row = lax.broadcasted_iota(jnp.int32, (BF16_SUBLANES, 4 * ATTN_BLOCK), 0) // SUBLANES
        slot_col = lax.broadcasted_iota(jnp.int32, (BF16_SUBLANES, 4 * ATTN_BLOCK), 1) // (
            2 * ATTN_BLOCK)
        ones_rows = jnp.where(slot_row == slot_col, 1.0, 0.0).astype(BF16)

        def scores(n, g, sub):
            cols = slice(g * LANES, (g + 1) * LANES)
            pair0 = g * PAIRS_PER_KV
            first = n * ATTN_BLOCK + sub * QUERY_SUB
            queries = pl.ds(first, QUERY_SUB)
            window = pl.ds(first, KEY_WIN)
            q_sub = jnp.concatenate(
                [q_ref[queries, (pair0 + p) * LANES:(pair0 + p + 1) * LANES]
                 for p in range(PAIRS_PER_KV)], axis=0)
            k_win = jnp.concatenate([k0_ref[window, cols], k1_ref[window, cols]], axis=0)
            return lax.dot_general(k_win, q_sub, (((1,), (1,)), ((), ())),
                                   preferred_element_type=F32)

        def finish(n, g, sub, s_t):
            pair0 = g * PAIRS_PER_KV
            band = slice(n * ATTN_BLOCK, (n + 2) * ATTN_BLOCK)
            vt = vt_ref[g * HEAD_DIM:(g + 1) * HEAD_DIM, band]
            queries = pl.ds(n * ATTN_BLOCK + sub * QUERY_SUB, QUERY_SUB)
            valid = in_window
            if n == 0:
                valid = valid & ((i > 0) | (key >= ATTN_BLOCK - sub * QUERY_SUB))
            probs = []
            for e in range(HEADS_PER_VREG):
                s = s_t[e * KEY_WIN:(e + 1) * KEY_WIN]
                sink = jnp.full((1, 2 * ATTN_BLOCK), NEG_INF, F32)
                for p in range(PAIRS_PER_KV):
                    head = (pair0 + p) * HEADS_PER_VREG + e
                    sink = jnp.where(lane // QUERY_SUB == p, sinks_ref[head] * LOG2_E, sink)
                top = jnp.where(valid[:SUBLANES], s[:SUBLANES],
                                jnp.where(key_top == 0, sink, NEG_INF))
                rest = jnp.where(valid[SUBLANES:], s[SUBLANES:], NEG_INF)
                s = jnp.concatenate([top, rest], axis=0)
                m = jnp.max(s, axis=0, keepdims=True)
                p_t = jnp.exp2(s - m).astype(BF16)
                probs += [p_t, zero_keys] if sub == 0 else [zero_keys, p_t]
            p_all = jnp.concatenate(probs, axis=0)
            keep = jnp.where(lane == sub * QUERY_SUB, 0.0, 1.0).astype(BF16)
            vt_keep = vt * keep
            v_aug = jnp.concatenate(
                [jnp.concatenate([vt_keep, zero_half], axis=1),
                 jnp.concatenate([zero_half, vt_keep], axis=1),
                 ones_rows], axis=0)
            o_t = _dot(v_aug, p_all)
            inv0 = 1.0 / o_t[2 * HEAD_DIM:2 * HEAD_DIM + 1]
            inv1 = 1.0 / o_t[2 * HEAD_DIM + SUBLANES:2 * HEAD_DIM + SUBLANES + 1]
            out_t = jnp.concatenate([o_t[:HEAD_DIM] * inv0,
                                     o_t[HEAD_DIM:2 * HEAD_DIM] * inv1], axis=0)
            out = jnp.transpose(out_t)
            for p in range(PAIRS_PER_KV):
                pair_cols = slice((pair0 + p) * LANES, (pair0 + p + 1) * LANES)
                o_ref[queries, pair_cols] = (
                    out[p * QUERY_SUB:(p + 1) * QUERY_SUB] * gate_ref[queries, pair_cols]
                ).astype(BF16)

        @pl.when(i >= 0)
        def _():
            work = [(n, g, sub) for n in range(tq // ATTN_BLOCK) for g in range(N_KV_HEADS)
                    for sub in range(ATTN_BLOCK // QUERY_SUB)]
            pending = []
            for step in range(len(work) + SCORE_LOOKAHEAD):
                if step < len(work):
                    pending.append((work[step], scores(*work[step])))
                if step >= SCORE_LOOKAHEAD:
                    item, s_t = pending.pop(0)
                    finish(*item, s_t)

        finish_previous_tile()
        r_ref[...] = DEEPNORM_ALPHA * x1 + _dot(o_ref[...], wout_ref[...])

    @pl.when(t < n_tiles)
    def _():
        attend_this_tile()

    @pl.when(t == n_tiles)
    def _():
        finish_previous_tile()


def _resident(shape):
    return pl.BlockSpec(shape, lambda *_: (0,) * len(shape), pipeline_mode=pl.Buffered(1))


def _rope_tables(tile, tiles_per_seq):
    lane = jnp.arange(LANES)
    inv_freq = ROPE_THETA ** (-(2 * (lane % HALF_DIM)).astype(F32) / HEAD_DIM)
    inside = jnp.arange(tile, dtype=F32)[:, None] * inv_freq[None, :]
    base = (jnp.arange(tiles_per_seq, dtype=F32) * tile)[:, None] * inv_freq[None, :]
    return jnp.cos(inside), jnp.sin(inside), jnp.cos(base), jnp.sin(base)


def _tile_maps(n_tiles, tiles_per_seq):
    def split(tile):
        return tile // tiles_per_seq, tile % tiles_per_seq

    def this_tile(t):
        return split(jnp.minimum(t, n_tiles - 1))

    def previous_tile(t):
        return split(jnp.maximum(t - 1, 0))

    return this_tile, previous_tile


def _layer_a(x, w_in, w_group, scale, w_out, ln_g, ln_b, w_k, w_v, rope_tables,
             next_w_qg, next_w_out):
    batch, seq, _ = x.shape
    tm = TILE_A
    tiles_per_seq = seq // tm
    n_tiles = batch * tiles_per_seq
    this_tile, previous_tile = _tile_maps(n_tiles, tiles_per_seq)
    lagged = lambda width: pl.BlockSpec((1, tm, width), lambda t: (*previous_tile(t), 0))
    in_hbm = pl.BlockSpec(memory_space=pl.ANY)
    cast_rows = D_MODEL // n_tiles
    assert cast_rows * n_tiles == D_MODEL and cast_rows % BF16_SUBLANES == 0
    cast_slice = lambda width: pl.BlockSpec(
        (cast_rows, width), lambda t: (jnp.minimum(t, n_tiles - 1), 0))
    return pl.pallas_call(
        functools.partial(_layer_a_kernel, n_tiles=n_tiles, tiles_per_seq=tiles_per_seq),
        grid=(n_tiles + 1,),
        in_specs=[
            pl.BlockSpec((1, tm, D_MODEL), lambda t: (*this_tile(t), 0)),
            in_hbm, in_hbm,
            _resident((1, D_MODEL)),
            in_hbm,
            _resident((1, D_MODEL)),
            _resident((1, D_MODEL)),
            in_hbm, in_hbm,
            _resident((tm, LANES)), _resident((tm, LANES)),
            _resident((tiles_per_seq, LANES)), _resident((tiles_per_seq, LANES)),
            cast_slice(2 * D_MODEL), cast_slice(D_MODEL),
        ],
        out_specs=[lagged(D_MODEL), lagged(KV_WIDTH), lagged(KV_WIDTH),
                   cast_slice(2 * D_MODEL), cast_slice(D_MODEL)],
        out_shape=[
            jax.ShapeDtypeStruct((batch, seq, D_MODEL), F32),
            jax.ShapeDtypeStruct((batch, seq, KV_WIDTH), BF16),
            jax.ShapeDtypeStruct((batch, seq, KV_WIDTH), BF16),
            jax.ShapeDtypeStruct((D_MODEL, 2 * D_MODEL), BF16),
            jax.ShapeDtypeStruct((D_MODEL, D_MODEL), BF16),
        ],
        scratch_shapes=[
            pltpu.VMEM((D_MODEL, D_MODEL), BF16),
            pltpu.VMEM((D_MODEL, D_MODEL), BF16),
            pltpu.VMEM((N_POOL_GROUPS, POOL_GROUP_DIM, POOL_GROUP_DIM), BF16),
            pltpu.VMEM((D_MODEL, D_MODEL), BF16),
            pltpu.VMEM((D_MODEL, 2 * KV_WIDTH), BF16),
            pltpu.VMEM((2, CAST_ROWS, CAST_COLS), F32),
            pltpu.SemaphoreType.DMA((2,)),
            pltpu.VMEM((N_POOL_GROUPS, tm, D_MODEL), BF16),
            pltpu.VMEM((tm, D_MODEL), F32),
            pltpu.VMEM((MAX_WINDOW, D_MODEL), F32),
            pltpu.VMEM((tm, D_MODEL), BF16),
            pltpu.VMEM((tm, D_MODEL), F32),
        ],
        compiler_params=pltpu.CompilerParams(
            dimension_semantics=("arbitrary",),
            vmem_limit_bytes=VMEM_LIMIT_BYTES),
        name="yoco_pool_layer",
    )(x, w_in, w_group, scale, w_out, ln_g, ln_b, w_k, w_v, *rope_tables, next_w_qg, next_w_out)


def _layer_b(x1, k, v, w_qg, sinks, w_out, ln_g, ln_b, rope_tables):
    batch, seq, _ = x1.shape
    tq = TILE_B
    blocks_per_tile = tq // ATTN_BLOCK
    tiles_per_seq = seq // tq
    n_tiles = batch * tiles_per_seq
    this_tile, previous_tile = _tile_maps(n_tiles, tiles_per_seq)
    tok = lambda width: pl.BlockSpec((1, tq, width), lambda t: (*this_tile(t), 0))

    def previous_block(t):
        b, i = this_tile(t)
        return b, jnp.maximum(i * blocks_per_tile - 1, 0), 0

    prev = pl.BlockSpec((1, ATTN_BLOCK, KV_WIDTH), previous_block)
    padded = N_KV_HEADS * LANES
    return pl.pallas_call(
        functools.partial(_layer_b_kernel, n_tiles=n_tiles, tiles_per_seq=tiles_per_seq),
        grid=(n_tiles + 1,),
        in_specs=[
            pl.BlockSpec(memory_space=pltpu.SMEM),
            tok(D_MODEL),
            tok(KV_WIDTH), prev, tok(KV_WIDTH), prev,
            _resident((D_MODEL, 2 * D_MODEL)),
            _resident((D_MODEL, D_MODEL)),
            _resident((1, D_MODEL)),
            _resident((1, D_MODEL)),
            _resident((tq, LANES)), _resident((tq, LANES)),
            _resident((tiles_per_seq, LANES)), _resident((tiles_per_seq, LANES)),
        ],
        out_specs=pl.BlockSpec((1, tq, D_MODEL), lambda t: (*previous_tile(t), 0)),
        out_shape=jax.ShapeDtypeStruct((batch, seq, D_MODEL), F32),
        scratch_shapes=[
            pltpu.VMEM((tq, D_MODEL), BF16),
            pltpu.VMEM((tq + ATTN_BLOCK, padded), BF16),
            pltpu.VMEM((tq + ATTN_BLOCK, padded), BF16),
            pltpu.VMEM((KV_WIDTH, tq + ATTN_BLOCK), BF16),
            pltpu.VMEM((tq, D_MODEL), BF16),
            pltpu.VMEM((tq, D_MODEL), F32),
            pltpu.VMEM((tq, D_MODEL), F32),
        ],
        compiler_params=pltpu.CompilerParams(
            dimension_semantics=("arbitrary",),
            vmem_limit_bytes=VMEM_LIMIT_BYTES),
        name="yoco_swa_layer",
    )(sinks, x1, k, k, v, v, w_qg, w_out, ln_g, ln_b, *rope_tables)


def kernel(x, ln_g, ln_b, a_w_in, a_w_group, a_scale, a_w_out, b_w_k, b_w_v, b_w_qg, b_sinks, b_w_out):
    assert a_w_in.shape[0] == 1 and b_w_qg.shape[0] == 1
    assert TILE_A == TILE_B
    rope_tables = _rope_tables(TILE_A, x.shape[1] // TILE_A)
    x1, k, v, w_qg_bf, b_w_out_bf = _layer_a(
        x, a_w_in[0], a_w_group[0], a_scale[0][None, :], a_w_out[0],
        ln_g[0][None, :], ln_b[0][None, :], b_w_k, b_w_v, rope_tables, b_w_qg[0], b_w_out[0])
    return _layer_b(
        x1, k, v, w_qg_bf, b_sinks[0], b_w_out_bf,
        ln_g[1][None, :], ln_b[1][None, :], rope_tables)
```

```python
import functools

import jax
import jax.numpy as jnp
from jax import lax
from jax.experimental import pallas as pl
from jax.experimental.pallas import tpu as pltpu

D_MODEL = 2048
DEPTH = 2
POOL_WINDOWS = (2, 4, 8, 16)
N_POOL_GROUPS = len(POOL_WINDOWS)
POOL_GROUP_DIM = D_MODEL // N_POOL_GROUPS
MAX_WINDOW = max(POOL_WINDOWS)
HEAD_DIM = 64
HALF_DIM = HEAD_DIM // 2
N_Q_HEADS = D_MODEL // HEAD_DIM
N_KV_HEADS = N_Q_HEADS // 8
GQA_GROUP = N_Q_HEADS // N_KV_HEADS
KV_WIDTH = N_KV_HEADS * HEAD_DIM
ATTN_BLOCK = 128
ROPE_THETA = 10000.0
LN_EPS = 1e-5
NEG_INF = -1e30
LOG2_E = 1.4426950408889634
DEEPNORM_ALPHA = (2 * DEPTH) ** 0.25

LANES = 128
SUBLANES = 8
BF16_SUBLANES = 16
QUERY_SUB = 64
KEY_WIN = QUERY_SUB + ATTN_BLOCK
SCORE_LOOKAHEAD = 3
HEADS_PER_VREG = LANES // HEAD_DIM
PAIRS_PER_KV = GQA_GROUP // HEADS_PER_VREG
VMEM_LIMIT_BYTES = 56 * 1024 * 1024
TILE_A = 256
TILE_B = 256
CAST_ROWS = 512
CAST_COLS = 512

F32 = jnp.float32
BF16 = jnp.bfloat16


def _dot(a, b):
    return jnp.dot(a, b, preferred_element_type=F32)


def _silu(z):
    half = 0.5 * z
    return half + half * jnp.tanh(half)


def _layer_norm(r, g, b):
    mu = jnp.mean(r, axis=-1, keepdims=True)
    c = r - mu
    var = jnp.mean(c * c, axis=-1, keepdims=True)
    return c * lax.rsqrt(var + LN_EPS) * g + b


def _tile_rope_tables(i, cos_in_ref, sin_in_ref, cos_base_ref, sin_base_ref):
    cb, sb = cos_base_ref[pl.ds(i, 1), :], sin_base_ref[pl.ds(i, 1), :]
    ci, si = cos_in_ref[...], sin_in_ref[...]
    lane = lax.broadcasted_iota(jnp.int32, (1, LANES), 1)
    sign = jnp.where((lane % HEAD_DIM) < HALF_DIM, -1.0, 1.0)
    return cb * ci - sb * si, (sb * ci + cb * si) * sign


def _rope(t, cos, sin_signed):
    width = t.shape[1]
    reps = width // LANES
    cos_w = jnp.concatenate([cos] * reps, axis=1)
    sin_w = jnp.concatenate([sin_signed] * reps, axis=1)
    lane = lax.broadcasted_iota(jnp.int32, (1, width), 1)
    first_half = (lane % HEAD_DIM) < HALF_DIM
    partner = jnp.where(first_half,
                        pltpu.roll(t, width - HALF_DIM, axis=1),
                        pltpu.roll(t, HALF_DIM, axis=1))
    return t * cos_w + partner * sin_w


def _cast_jobs(win_hbm, wg_hbm, wout_hbm, wk_hbm, wv_hbm, scale_ref,
               wmix_ref, wz_ref, wg_ref, wout_ref, wkv_ref):
    def fold(g):
        cols = slice(g * POOL_GROUP_DIM, (g + 1) * POOL_GROUP_DIM)
        return lambda chunk: (_dot(chunk, wg_ref[g]) * scale_ref[:, cols]).astype(BF16)

    jobs = [(wg_hbm.at[g], wg_ref.at[g], None) for g in range(N_POOL_GROUPS)]
    for r in range(0, D_MODEL, CAST_ROWS):
        for g in range(N_POOL_GROUPS):
            rows, cols = pl.ds(r, CAST_ROWS), pl.ds(g * POOL_GROUP_DIM, POOL_GROUP_DIM)
            jobs.append((win_hbm.at[rows, cols], wmix_ref.at[rows, cols], fold(g)))
    for src, col0, dst in ((win_hbm, D_MODEL, wz_ref), (wout_hbm, 0, wout_ref)):
        for r in range(0, D_MODEL, CAST_ROWS):
            for c in range(0, D_MODEL, CAST_COLS):
                rows = pl.ds(r, CAST_ROWS)
                jobs.append((src.at[rows, pl.ds(col0 + c, CAST_COLS)],
                             dst.at[rows, pl.ds(c, CAST_COLS)], None))
    for j, src in enumerate((wk_hbm, wv_hbm)):
        for r in range(0, D_MODEL, CAST_ROWS):
            jobs.append((src.at[pl.ds(r, CAST_ROWS), :],
                         wkv_ref.at[pl.ds(r, CAST_ROWS), pl.ds(j * KV_WIDTH, KV_WIDTH)], None))
    return jobs


def _load_and_cast_weights(jobs, stage_ref, sem_ref):
    def copy(k):
        src, dst, _ = jobs[k]
        rows, cols = dst.shape
        slot = k % 2
        return pltpu.make_async_copy(
            src, stage_ref.at[slot, pl.ds(0, rows), pl.ds(0, cols)], sem_ref.at[slot])

    copy(0).start()
    for k, (_, dst, transform) in enumerate(jobs):
        if k + 1 < len(jobs):
            copy(k + 1).start()
        copy(k).wait()
        rows, cols = dst.shape
        chunk = stage_ref[k % 2, pl.ds(0, rows), pl.ds(0, cols)].astype(BF16)
        dst[...] = chunk if transform is None else transform(chunk)


def _layer_a_kernel(x_ref, win_hbm, wg_hbm, scale_ref, wout_hbm, g_ref, b_ref, wk_hbm, wv_hbm,
                    cos_in_ref, sin_in_ref, cos_base_ref, sin_base_ref,
                    next_wqg_ref, next_wout_ref,
                    x1_ref, k_ref, v_ref, next_wqg_bf_ref, next_wout_bf_ref,
                    wmix_ref, wz_ref, wg_ref, wout_ref, wkv_ref, stage_ref, sem_ref,
                    pooled_ref, gate_ref, carry_ref, y_ref, r_ref, *, n_tiles, tiles_per_seq):
    tm = x_ref.shape[1]
    t = pl.program_id(0)

    @pl.when(t == 0)
    def _():
        _load_and_cast_weights(
            _cast_jobs(win_hbm, wg_hbm, wout_hbm, wk_hbm, wv_hbm, scale_ref,
                       wmix_ref, wz_ref, wg_ref, wout_ref, wkv_ref), stage_ref, sem_ref)
        r_ref[...] = jnp.zeros_like(r_ref)
        carry_ref[...] = jnp.zeros_like(carry_ref)

    def normalise_previous_tile():
        next_wqg_bf_ref[...] = next_wqg_ref[...].astype(BF16)
        next_wout_bf_ref[...] = next_wout_ref[...].astype(BF16)
        x1 = _layer_norm(r_ref[...], g_ref[...], b_ref[...])
        x1_ref[0] = x1
        return x1.astype(BF16)

    def project_kv_previous_tile(x1_bf):
        kv = _dot(x1_bf, wkv_ref[...])
        i_prev = jnp.maximum(t - 1, 0) % tiles_per_seq
        k = _rope(kv[:, :KV_WIDTH], *_tile_rope_tables(
            i_prev, cos_in_ref, sin_in_ref, cos_base_ref, sin_base_ref))
        k_ref[0] = k.astype(BF16)
        v_ref[0] = kv[:, KV_WIDTH:].astype(BF16)

    def mix_this_tile():
        i = t % tiles_per_seq
        x = x_ref[0]
        xb = x.astype(BF16)
        pos1 = lax.broadcasted_iota(jnp.int32, (tm, 1), 0) + (i * tm + 1)
        gate_ref[...] = _silu(_dot(xb, wz_ref[...]))
        for c in range(0, D_MODEL, POOL_GROUP_DIM):
            cols = slice(c, c + POOL_GROUP_DIM)
            history = jnp.where(i > 0, carry_ref[:, cols], 0.0)
            acc = jnp.concatenate([history, x[:, cols]], axis=0)
            shift = 1
            for g, w in enumerate(POOL_WINDOWS):
                while shift < w:
                    acc = acc + pltpu.roll(acc, shift, axis=0)
                    shift *= 2
                inv_count = 1.0 / jnp.minimum(pos1, w).astype(F32)
                pooled_ref[g, :, cols] = (acc[MAX_WINDOW:, :] * inv_count - x[:, cols]).astype(BF16)
        carry_ref[...] = x[tm - MAX_WINDOW:, :]
        for g in range(N_POOL_GROUPS):
            lo, hi = g * POOL_GROUP_DIM, (g + 1) * POOL_GROUP_DIM
            mixed = _dot(pooled_ref[g], wmix_ref[:, lo:hi])
            y_ref[:, lo:hi] = (mixed * gate_ref[:, lo:hi]).astype(BF16)
        r_ref[...] = DEEPNORM_ALPHA * x + _dot(y_ref[...], wout_ref[...])

    @pl.when(t < n_tiles)
    def _():
        x1_bf = normalise_previous_tile()
        mix_this_tile()
        project_kv_previous_tile(x1_bf)

    @pl.when(t == n_tiles)
    def _():
        project_kv_previous_tile(normalise_previous_tile())


def _pad_heads(t):
    lane = lax.broadcasted_iota(jnp.int32, (1, LANES), 1)
    low = lane < HEAD_DIM
    lo_cols, hi_cols = [], []
    for j in range(KV_WIDTH // LANES):
        col = t[:, j * LANES:(j + 1) * LANES]
        swapped = pltpu.roll(col, HEAD_DIM, axis=1)
        lo_cols += [jnp.where(low, col, 0.0), jnp.where(low, swapped, 0.0)]
        hi_cols += [jnp.where(low, 0.0, swapped), jnp.where(low, 0.0, col)]
    return (jnp.concatenate(lo_cols, axis=1).astype(BF16),
            jnp.concatenate(hi_cols, axis=1).astype(BF16))


def _layer_b_kernel(sinks_ref, x1_ref, kc_ref, kp_ref, vc_ref, vp_ref, wqg_ref, wout_ref,
                    g_ref, b_ref, cos_in_ref, sin_in_ref, cos_base_ref, sin_base_ref, out_ref,
                    q_ref, k0_ref, k1_ref, vt_ref, o_ref, gate_ref, r_ref,
                    *, n_tiles, tiles_per_seq):
    tq = x1_ref.shape[1]
    t = pl.program_id(0)

    @pl.when(t == 0)
    def _():
        r_ref[...] = jnp.zeros_like(r_ref)

    def finish_previous_tile(rows=slice(None)):
        out_ref[0, rows, :] = _layer_norm(r_ref[rows, :], g_ref[...], b_ref[...])

    def attend_this_tile():
        i = t % tiles_per_seq
        x1 = x1_ref[0]
        xb = x1.astype(BF16)

        cos, sin_signed = _tile_rope_tables(i, cos_in_ref, sin_in_ref, cos_base_ref, sin_base_ref)
        q_scale = HEAD_DIM ** -0.5 * LOG2_E
        q = _rope(_dot(xb, wqg_ref[:, :D_MODEL]), cos * q_scale, sin_signed * q_scale)
        q_ref[...] = q.astype(BF16)

        kf = jnp.concatenate([kp_ref[0], kc_ref[0]], axis=0).astype(F32)
        vf = jnp.concatenate([vp_ref[0], vc_ref[0]], axis=0).astype(F32)
        k0_ref[...], k1_ref[...] = _pad_heads(kf)
        vt_ref[...] = jnp.transpose(vf).astype(BF16)

        gate_ref[...] = _silu(_dot(xb, wqg_ref[:, D_MODEL:]))

        key = lax.broadcasted_iota(jnp.int32, (KEY_WIN, 2 * ATTN_BLOCK), 0)
        qry = lax.broadcasted_iota(jnp.int32, (KEY_WIN, 2 * ATTN_BLOCK), 1) % QUERY_SUB
        in_window = (key > qry) & (key <= qry + ATTN_BLOCK)
        lane = lax.broadcasted_iota(jnp.int32, (1, 2 * ATTN_BLOCK), 1)
        key_top = lax.broadcasted_iota(jnp.int32, (SUBLANES, 2 * ATTN_BLOCK), 0)
        zero_keys = jnp.zeros((QUERY_SUB, 2 * ATTN_BLOCK), BF16)
        zero_half = jnp.zeros((HEAD_DIM, 2 * ATTN_BLOCK), BF16)
        slot_row = lax.broadcasted_iota(jnp.int32, (BF16_SUBLANES, 4 * ATTN_BLOCK), 0) // SUBLANES
        slot_col = lax.broadcasted_iota(jnp.int32, (BF16_SUBLANES, 4 * ATTN_BLOCK), 1) // (
            2 * ATTN_BLOCK)
        ones_rows = jnp.where(slot_row == slot_col, 1.0, 0.0).astype(BF16)

        def scores(n, g, sub):
            cols = slice(g * LANES, (g + 1) * LANES)
            pair0 = g * PAIRS_PER_KV
            first = n * ATTN_BLOCK + sub * QUERY_SUB
            queries = pl.ds(first, QUERY_SUB)
            window = pl.ds(first, KEY_WIN)
            q_sub = jnp.concatenate(
                [q_ref[queries, (pair0 + p) * LANES:(pair0 + p + 1) * LANES]
                 for p in range(PAIRS_PER_KV)], axis=0)
            k_win = jnp.concatenate([k0_ref[window, cols], k1_ref[window, cols]], axis=0)
            return lax.dot_general(k_win, q_sub, (((1,), (1,)), ((), ())),
                                   preferred_element_type=F32)

        def finish(n, g, sub, s_t):
            pair0 = g * PAIRS_PER_KV
            band = slice(n * ATTN_BLOCK, (n + 2) * ATTN_BLOCK)
            vt = vt_ref[g * HEAD_DIM:(g + 1) * HEAD_DIM, band]
            queries = pl.ds(n * ATTN_BLOCK + sub * QUERY_SUB, QUERY_SUB)
            valid = in_window
            if n == 0:
                valid = valid & ((i > 0) | (key >= ATTN_BLOCK - sub * QUERY_SUB))
            probs = []
            for e in range(HEADS_PER_VREG):
                s = s_t[e * KEY_WIN:(e + 1) * KEY_WIN]
                sink = jnp.full((1, 2 * ATTN_BLOCK), NEG_INF, F32)
                for p in range(PAIRS_PER_KV):
                    head = (pair0 + p) * HEADS_PER_VREG + e
                    sink = jnp.where(lane // QUERY_SUB == p, sinks_ref[head] * LOG2_E, sink)
                top = jnp.where(valid[:SUBLANES], s[:SUBLANES],
                                jnp.where(key_top == 0, sink, NEG_INF))
                rest = jnp.where(valid[SUBLANES:], s[SUBLANES:], NEG_INF)
                s = jnp.concatenate([top, rest], axis=0)
                m = jnp.max(s, axis=0, keepdims=True)
                p_t = jnp.exp2(s - m).astype(BF16)
                probs += [p_t, zero_keys] if sub == 0 else [zero_keys, p_t]
            p_all = jnp.concatenate(probs, axis=0)
            keep = jnp.where(lane == sub * QUERY_SUB, 0.0, 1.0).astype(BF16)
            vt_keep = vt * keep
            v_aug = jnp.concatenate(
                [jnp.concatenate([vt_keep, zero_half], axis=1),
                 jnp.concatenate([zero_half, vt_keep], axis=1),
                 ones_rows], axis=0)
            o_t = _dot(v_aug, p_all)
            inv0 = 1.0 / o_t[2 * HEAD_DIM:2 * HEAD_DIM + 1]
            inv1 = 1.0 / o_t[2 * HEAD_DIM + SUBLANES:2 * HEAD_DIM + SUBLANES + 1]
            out_t = jnp.concatenate([o_t[:HEAD_DIM] * inv0,
                                     o_t[HEAD_DIM:2 * HEAD_DIM] * inv1], axis=0)
            out = jnp.transpose(out_t)
            for p in range(PAIRS_PER_KV):
                pair_cols = slice((pair0 + p) * LANES, (pair0 + p + 1) * LANES)
                o_ref[queries, pair_cols] = (
                    out[p * QUERY_SUB:(p + 1) * QUERY_SUB] * gate_ref[queries, pair_cols]
                ).astype(BF16)

        @pl.when(i >= 0)
        def _():
            work = [(n, g, sub) for n in range(tq // ATTN_BLOCK) for g in range(N_KV_HEADS)
                    for sub in range(ATTN_BLOCK // QUERY_SUB)]
            pending = []
            for step in range(len(work) + SCORE_LOOKAHEAD):
                if step < len(work):
                    pending.append((work[step], scores(*work[step])))
                if step >= SCORE_LOOKAHEAD:
                    item, s_t = pending.pop(0)
                    finish(*item, s_t)

        finish_previous_tile()
        r_ref[...] = DEEPNORM_ALPHA * x1 + _dot(o_ref[...], wout_ref[...])

    @pl.when(t < n_tiles)
    def _():
        attend_this_tile()

    @pl.when(t == n_tiles)
    def _():
        finish_previous_tile()


def _resident(shape):
    return pl.BlockSpec(shape, lambda *_: (0,) * len(shape), pipeline_mode=pl.Buffered(1))


def _rope_tables(tile, tiles_per_seq):
    lane = jnp.arange(LANES)
    inv_freq = ROPE_THETA ** (-(2 * (lane % HALF_DIM)).astype(F32) / HEAD_DIM)
    inside = jnp.arange(tile, dtype=F32)[:, None] * inv_freq[None, :]
    base = (jnp.arange(tiles_per_seq, dtype=F32) * tile)[:, None] * inv_freq[None, :]
    return jnp.cos(inside), jnp.sin(inside), jnp.cos(base), jnp.sin(base)


def _tile_maps(n_tiles, tiles_per_seq):
    def split(tile):
        return tile // tiles_per_seq, tile % tiles_per_seq

    def this_tile(t):
        return split(jnp.minimum(t, n_tiles - 1))

    def previous_tile(t):
        return split(jnp.maximum(t - 1, 0))

    return this_tile, previous_tile


def _layer_a(x, w_in, w_group, scale, w_out, ln_g, ln_b, w_k, w_v, rope_tables,
             next_w_qg, next_w_out):
    batch, seq, _ = x.shape
    tm = TILE_A
    tiles_per_seq = seq // tm
    n_tiles = batch * tiles_per_seq
    this_tile, previous_tile = _tile_maps(n_tiles, tiles_per_seq)
    lagged = lambda width: pl.BlockSpec((1, tm, width), lambda t: (*previous_tile(t), 0))
    in_hbm = pl.BlockSpec(memory_space=pl.ANY)
    cast_rows = D_MODEL // n_tiles
    assert cast_rows * n_tiles == D_MODEL and cast_rows % BF16_SUBLANES == 0
    cast_slice = lambda width: pl.BlockSpec(
        (cast_rows, width), lambda t: (jnp.minimum(t, n_tiles - 1), 0))
    return pl.pallas_call(
        functools.partial(_layer_a_kernel, n_tiles=n_tiles, tiles_per_seq=tiles_per_seq),
        grid=(n_tiles + 1,),
        in_specs=[
            pl.BlockSpec((1, tm, D_MODEL), lambda t: (*this_tile(t), 0)),
            in_hbm, in_hbm,
            _resident((1, D_MODEL)),
            in_hbm,
            _resident((1, D_MODEL)),
            _resident((1, D_MODEL)),
            in_hbm, in_hbm,
            _resident((tm, LANES)), _resident((tm, LANES)),
            _resident((tiles_per_seq, LANES)), _resident((tiles_per_seq, LANES)),
            cast_slice(2 * D_MODEL), cast_slice(D_MODEL),
        ],
        out_specs=[lagged(D_MODEL), lagged(KV_WIDTH), lagged(KV_WIDTH),
                   cast_slice(2 * D_MODEL), cast_slice(D_MODEL)],
        out_shape=[
            jax.ShapeDtypeStruct((batch, seq, D_MODEL), F32),
            jax.ShapeDtypeStruct((batch, seq, KV_WIDTH), BF16),
            jax.ShapeDtypeStruct((batch, seq, KV_WIDTH), BF16),
            jax.ShapeDtypeStruct((D_MODEL, 2 * D_MODEL), BF16),
            jax.ShapeDtypeStruct((D_MODEL, D_MODEL), BF16),
        ],
        scratch_shapes=[
            pltpu.VMEM((D_MODEL, D_MODEL), BF16),
            pltpu.VMEM((D_MODEL, D_MODEL), BF16),
            pltpu.VMEM((N_POOL_GROUPS, POOL_GROUP_DIM, POOL_GROUP_DIM), BF16),
            pltpu.VMEM((D_MODEL, D_MODEL), BF16),
            pltpu.VMEM((D_MODEL, 2 * KV_WIDTH), BF16),
            pltpu.VMEM((2, CAST_ROWS, CAST_COLS), F32),
            pltpu.SemaphoreType.DMA((2,)),
            pltpu.VMEM((N_POOL_GROUPS, tm, D_MODEL), BF16),
            pltpu.VMEM((tm, D_MODEL), F32),
            pltpu.VMEM((MAX_WINDOW, D_MODEL), F32),
            pltpu.VMEM((tm, D_MODEL), BF16),
            pltpu.VMEM((tm, D_MODEL), F32),
        ],
        compiler_params=pltpu.CompilerParams(
            dimension_semantics=("arbitrary",),
            vmem_limit_bytes=VMEM_LIMIT_BYTES),
        name="yoco_pool_layer",
    )(x, w_in, w_group, scale, w_out, ln_g, ln_b, w_k, w_v, *rope_tables, next_w_qg, next_w_out)


def _layer_b(x1, k, v, w_qg, sinks, w_out, ln_g, ln_b, rope_tables):
    batch, seq, _ = x1.shape
    tq = TILE_B
    blocks_per_tile = tq // ATTN_BLOCK
    tiles_per_seq = seq // tq
    n_tiles = batch * tiles_per_seq
    this_tile, previous_tile = _tile_maps(n_tiles, tiles_per_seq)
    tok = lambda width: pl.BlockSpec((1, tq, width), lambda t: (*this_tile(t), 0))

    def previous_block(t):
        b, i = this_tile(t)
        return b, jnp.maximum(i * blocks_per_tile - 1, 0), 0

    prev = pl.BlockSpec((1, ATTN_BLOCK, KV_WIDTH), previous_block)
    padded = N_KV_HEADS * LANES
    return pl.pallas_call(
        functools.partial(_layer_b_kernel, n_tiles=n_tiles, tiles_per_seq=tiles_per_seq),
        grid=(n_tiles + 1,),
        in_specs=[
            pl.BlockSpec(memory_space=pltpu.SMEM),
            tok(D_MODEL),
            tok(KV_WIDTH), prev, tok(KV_WIDTH), prev,
            _resident((D_MODEL, 2 * D_MODEL)),
            _resident((D_MODEL, D_MODEL)),
            _resident((1, D_MODEL)),
            _resident((1, D_MODEL)),
            _resident((tq, LANES)), _resident((tq, LANES)),
            _resident((tiles_per_seq, LANES)), _resident((tiles_per_seq, LANES)),
        ],
        out_specs=pl.BlockSpec((1, tq, D_MODEL), lambda t: (*previous_tile(t), 0)),
        out_shape=jax.ShapeDtypeStruct((batch, seq, D_MODEL), F32),
        scratch_shapes=[
            pltpu.VMEM((tq, D_MODEL), BF16),
            pltpu.VMEM((tq + ATTN_BLOCK, padded), BF16),
            pltpu.VMEM((tq + ATTN_BLOCK, padded), BF16),
            pltpu.VMEM((KV_WIDTH, tq + ATTN_BLOCK), BF16),
            pltpu.VMEM((tq, D_MODEL), BF16),
            pltpu.VMEM((tq, D_MODEL), F32),
            pltpu.VMEM((tq, D_MODEL), F32),
        ],
        compiler_params=pltpu.CompilerParams(
            dimension_semantics=("arbitrary",),
            vmem_limit_bytes=VMEM_LIMIT_BYTES),
        name="yoco_swa_layer",
    )(sinks, x1, k, k, v, v, w_qg, w_out, ln_g, ln_b, *rope_tables)


def kernel(x, ln_g, ln_b, a_w_in, a_w_group, a_scale, a_w_out, b_w_k, b_w_v, b_w_qg, b_sinks, b_w_out):
    assert a_w_in.shape[0] == 1 and b_w_qg.shape[0] == 1
    assert TILE_A == TILE_B
    rope_tables = _rope_tables(TILE_A, x.shape[1] // TILE_A)
    x1, k, v, w_qg_bf, b_w_out_bf = _layer_a(
        x, a_w_in[0], a_w_group[0], a_scale[0][None, :], a_w_out[0],
        ln_g[0][None, :], ln_b[0][None, :], b_w_k, b_w_v, rope_tables, b_w_qg[0], b_w_out[0])
    return _layer_b(
        x1, k, v, w_qg_bf, b_sinks[0], b_w_out_bf,
        ln_g[1][None, :], ln_b[1][None, :], rope_tables)
```

```python
import functools

import jax
import jax.numpy as jnp
from jax import lax
from jax.experimental import pallas as pl
from jax.experimental.pallas import tpu as pltpu

D_MODEL = 2048
DEPTH = 2
POOL_WINDOWS = (2, 4, 8, 16)
N_POOL_GROUPS = len(POOL_WINDOWS)
POOL_GROUP_DIM = D_MODEL // N_POOL_GROUPS
MAX_WINDOW = max(POOL_WINDOWS)
HEAD_DIM = 64
HALF_DIM = HEAD_DIM // 2
N_Q_HEADS = D_MODEL // HEAD_DIM
N_KV_HEADS = N_Q_HEADS // 8
GQA_GROUP = N_Q_HEADS // N_KV_HEADS
KV_WIDTH = N_KV_HEADS * HEAD_DIM
ATTN_BLOCK = 128
ROPE_THETA = 10000.0
LN_EPS = 1e-5
NEG_INF = -1e30
LOG2_E = 1.4426950408889634
DEEPNORM_ALPHA = (2 * DEPTH) ** 0.25

LANES = 128
SUBLANES = 8
BF16_SUBLANES = 16
QUERY_SUB = 64
KEY_WIN = QUERY_SUB + ATTN_BLOCK
SCORE_LOOKAHEAD = 3
HEADS_PER_VREG = LANES // HEAD_DIM
PAIRS_PER_KV = GQA_GROUP // HEADS_PER_VREG
VMEM_LIMIT_BYTES = 56 * 1024 * 1024
TILE_A = 256
TILE_B = 256
CAST_ROWS = 512
CAST_COLS = 512

F32 = jnp.float32
BF16 = jnp.bfloat16


def _dot(a, b):
    return jnp.dot(a, b, preferred_element_type=F32)


def _silu(z):
    half = 0.5 * z
    return half + half * jnp.tanh(half)


def _layer_norm(r, g, b):
    mu = jnp.mean(r, axis=-1, keepdims=True)
    c = r - mu
    var = jnp.mean(c * c, axis=-1, keepdims=True)
    return c * lax.rsqrt(var + LN_EPS / DEEPNORM_ALPHA ** 2) * g + b


def _tile_rope_tables(i, cos_in_ref, sin_in_ref, cos_base_ref, sin_base_ref):
    cb, sb = cos_base_ref[pl.ds(i, 1), :], sin_base_ref[pl.ds(i, 1), :]
    ci, si = cos_in_ref[...], sin_in_ref[...]
    lane = lax.broadcasted_iota(jnp.int32, (1, LANES), 1)
    sign = jnp.where((lane % HEAD_DIM) < HALF_DIM, -1.0, 1.0)
    return cb * ci - sb * si, (sb * ci + cb * si) * sign


def _rope(t, cos, sin_signed):
    width = t.shape[1]
    reps = width // LANES
    cos_w = jnp.concatenate([cos] * reps, axis=1)
    sin_w = jnp.concatenate([sin_signed] * reps, axis=1)
    lane = lax.broadcasted_iota(jnp.int32, (1, width), 1)
    first_half = (lane % HEAD_DIM) < HALF_DIM
    partner = jnp.where(first_half,
                        pltpu.roll(t, width - HALF_DIM, axis=1),
                        pltpu.roll(t, HALF_DIM, axis=1))
    return t * cos_w + partner * sin_w


def _cast_jobs(win_hbm, wg_hbm, wout_hbm, wk_hbm, wv_hbm, scale_ref,
               wmix_ref, wz_ref, wg_ref, wout_ref, wkv_ref):
    def fold(g):
        cols = slice(g * POOL_GROUP_DIM, (g + 1) * POOL_GROUP_DIM)
        return lambda chunk: (_dot(chunk, wg_ref[g]) * scale_ref[:, cols]).astype(BF16)

    jobs = [(wg_hbm.at[g], wg_ref.at[g], None) for g in range(N_POOL_GROUPS)]
    for r in range(0, D_MODEL, CAST_ROWS):
        for g in range(N_POOL_GROUPS):
            rows, cols = pl.ds(r, CAST_ROWS), pl.ds(g * POOL_GROUP_DIM, POOL_GROUP_DIM)
            jobs.append((win_hbm.at[rows, cols], wmix_ref.at[rows, cols], fold(g)))
    for src, col0, dst, scale in ((win_hbm, D_MODEL, wz_ref, None),
                                  (wout_hbm, 0, wout_ref, 1.0 / DEEPNORM_ALPHA)):
        for r in range(0, D_MODEL, CAST_ROWS):
            for c in range(0, D_MODEL, CAST_COLS):
                rows = pl.ds(r, CAST_ROWS)
                jobs.append((src.at[rows, pl.ds(col0 + c, CAST_COLS)],
                             dst.at[rows, pl.ds(c, CAST_COLS)], scale))
    for j, src in enumerate((wk_hbm, wv_hbm)):
        for r in range(0, D_MODEL, CAST_ROWS):
            jobs.append((src.at[pl.ds(r, CAST_ROWS), :],
                         wkv_ref.at[pl.ds(r, CAST_ROWS), pl.ds(j * KV_WIDTH, KV_WIDTH)], None))
    return jobs


def _load_and_cast_weights(jobs, stage_ref, sem_ref):
    def copy(k):
        src, dst, _ = jobs[k]
        rows, cols = dst.shape
        slot = k % 2
        return pltpu.make_async_copy(
            src, stage_ref.at[slot, pl.ds(0, rows), pl.ds(0, cols)], sem_ref.at[slot])

    copy(0).start()
    for k, (_, dst, transform) in enumerate(jobs):
        if k + 1 < len(jobs):
            copy(k + 1).start()
        copy(k).wait()
        rows, cols = dst.shape
        chunk = stage_ref[k % 2, pl.ds(0, rows), pl.ds(0, cols)]
        if isinstance(transform, float):
            dst[...] = (chunk * transform).astype(BF16)
        elif transform is None:
            dst[...] = chunk.astype(BF16)
        else:
            dst[...] = transform(chunk.astype(BF16))


def _layer_a_kernel(x_ref, win_hbm, wg_hbm, scale_ref, wout_hbm, g_ref, b_ref, wk_hbm, wv_hbm,
                    cos_in_ref, sin_in_ref, cos_base_ref, sin_base_ref,
                    next_wqg_ref, next_wout_ref,
                    x1_ref, k_ref, v_ref, next_wqg_bf_ref, next_wout_bf_ref,
                    wmix_ref, wz_ref, wg_ref, wout_ref, wkv_ref, stage_ref, sem_ref,
                    pooled_ref, gate_ref, carry_ref, y_ref, r_ref, *, n_tiles, tiles_per_seq):
    tm = x_ref.shape[1]
    t = pl.program_id(0)

    @pl.when(t == 0)
    def _():
        _load_and_cast_weights(
            _cast_jobs(win_hbm, wg_hbm, wout_hbm, wk_hbm, wv_hbm, scale_ref,
                       wmix_ref, wz_ref, wg_ref, wout_ref, wkv_ref), stage_ref, sem_ref)
        r_ref[...] = jnp.zeros_like(r_ref)
        carry_ref[...] = jnp.zeros_like(carry_ref)

    def normalise_previous_tile():
        next_wqg_bf_ref[...] = next_wqg_ref[...].astype(BF16)
        next_wout_bf_ref[...] = (next_wout_ref[...] * (1.0 / DEEPNORM_ALPHA)).astype(BF16)
        x1 = _layer_norm(r_ref[...], g_ref[...], b_ref[...])
        x1_ref[0] = x1
        return x1.astype(BF16)

    def project_kv_previous_tile(x1_bf):
        kv = _dot(x1_bf, wkv_ref[...])
        i_prev = jnp.maximum(t - 1, 0) % tiles_per_seq
        k = _rope(kv[:, :KV_WIDTH], *_tile_rope_tables(
            i_prev, cos_in_ref, sin_in_ref, cos_base_ref, sin_base_ref))
        k_ref[0] = k.astype(BF16)
        v_ref[0] = kv[:, KV_WIDTH:].astype(BF16)

    def mix_this_tile():
        i = t % tiles_per_seq
        x = x_ref[0]
        xb = x.astype(BF16)
        pos1 = lax.broadcasted_iota(jnp.int32, (tm, 1), 0) + (i * tm + 1)
        gate_ref[...] = _silu(_dot(xb, wz_ref[...]))
        for c in range(0, D_MODEL, POOL_GROUP_DIM):
            cols = slice(c, c + POOL_GROUP_DIM)
            history = jnp.where(i > 0, carry_ref[:, cols], 0.0)
            acc = jnp.concatenate([history, x[:, cols]], axis=0)
            shift = 1
            for g, w in enumerate(POOL_WINDOWS):
                while shift < w:
                    acc = acc + pltpu.roll(acc, shift, axis=0)
                    shift *= 2
                inv_count = 1.0 / jnp.minimum(pos1, w).astype(F32)
                pooled_ref[g, :, cols] = (acc[MAX_WINDOW:, :] * inv_count - x[:, cols]).astype(BF16)
        carry_ref[...] = x[tm - MAX_WINDOW:, :]
        for g in range(N_POOL_GROUPS):
            lo, hi = g * POOL_GROUP_DIM, (g + 1) * POOL_GROUP_DIM
            mixed = _dot(pooled_ref[g], wmix_ref[:, lo:hi])
            y_ref[:, lo:hi] = (mixed * gate_ref[:, lo:hi]).astype(BF16)
        r_ref[...] = x + _dot(y_ref[...], wout_ref[...])

    @pl.when(t < n_tiles)
    def _():
        x1_bf = normalise_previous_tile()
        mix_this_tile()
        project_kv_previous_tile(x1_bf)

    @pl.when(t == n_tiles)
    def _():
        project_kv_previous_tile(normalise_previous_tile())


def _pad_heads(t):
    lane = lax.broadcasted_iota(jnp.int32, (1, LANES), 1)
    low = lane < HEAD_DIM
    lo_cols, hi_cols = [], []
    for j in range(KV_WIDTH // LANES):
        col = t[:, j * LANES:(j + 1) * LANES]
        swapped = pltpu.roll(col, HEAD_DIM, axis=1)
        lo_cols += [jnp.where(low, col, 0.0), jnp.where(low, swapped, 0.0)]
        hi_cols += [jnp.where(low, 0.0, swapped), jnp.where(low, 0.0, col)]
    return (jnp.concatenate(lo_cols, axis=1).astype(BF16),
            jnp.concatenate(hi_cols, axis=1).astype(BF16))


def _layer_b_kernel(sinks_ref, x1_ref, kc_ref, kp_ref, vc_ref, vp_ref, wqg_ref, wout_ref,
                    g_ref, b_ref, cos_in_ref, sin_in_ref, cos_base_ref, sin_base_ref, out_ref,
                    q_ref, k0_ref, k1_ref, vt_ref, o_ref, gate_ref, r_ref,
                    *, n_tiles, tiles_per_seq):
    tq = x1_ref.shape[1]
    t = pl.program_id(0)

    @pl.when(t == 0)
    def _():
        r_ref[...] = jnp.zeros_like(r_ref)

    def finish_previous_tile(rows=slice(None)):
        out_ref[0, rows, :] = _layer_norm(r_ref[rows, :], g_ref[...], b_ref[...])

    def attend_this_tile():
        i = t % tiles_per_seq
        x1 = x1_ref[0]
        xb = x1.astype(BF16)

        cos, sin_signed = _tile_rope_tables(i, cos_in_ref, sin_in_ref, cos_base_ref, sin_base_ref)
        q_scale = HEAD_DIM ** -0.5 * LOG2_E
        q = _rope(_dot(xb, wqg_ref[:, :D_MODEL]), cos * q_scale, sin_signed * q_scale)
        q_ref[...] = q.astype(BF16)

        kf = jnp.concatenate([kp_ref[0], kc_ref[0]], axis=0).astype(F32)
        vf = jnp.concatenate([vp_ref[0], vc_ref[0]], axis=0).astype(F32)
        k0_ref[...], k1_ref[...] = _pad_heads(kf)
        vt_ref[...] = jnp.transpose(vf).astype(BF16)

        gate_ref[...] = _silu(_dot(xb, wqg_ref[:, D_MODEL:]))

        key = lax.broadcasted_iota(jnp.int32, (KEY_WIN, 2 * ATTN_BLOCK), 0)
        qry = lax.broadcasted_iota(jnp.int32, (KEY_WIN, 2 * ATTN_BLOCK), 1) % QUERY_SUB
        in_window = (key > qry) & (key <= qry + ATTN_BLOCK)
        lane = lax.broadcasted_iota(jnp.int32, (1, 2 * ATTN_BLOCK), 1)
        key_top = lax.broadcasted_iota(jnp.int32, (SUBLANES, 2 * ATTN_BLOCK), 0)
        zero_keys = jnp.zeros((QUERY_SUB, 2 * ATTN_BLOCK), BF16)
        zero_half = jnp.zeros((HEAD_DIM, 2 * ATTN_BLOCK), BF16)
        slot_row = lax.broadcasted_iota(jnp.int32, (BF16_SUBLANES, 4 * ATTN_BLOCK), 0) // SUBLANES
        slot_col = lax.broadcasted_iota(jnp.int32, (BF16_SUBLANES, 4 * ATTN_BLOCK), 1) // (
            2 * ATTN_BLOCK)
        ones_rows = jnp.where(slot_row == slot_col, 1.0, 0.0).astype(BF16)

        def scores(n, g, sub):
            cols = slice(g * LANES, (g + 1) * LANES)
            pair0 = g * PAIRS_PER_KV
            first = n * ATTN_BLOCK + sub * QUERY_SUB
            queries = pl.ds(first, QUERY_SUB)
            window = pl.ds(first, KEY_WIN)
            q_sub = jnp.concatenate(
                [q_ref[queries, (pair0 + p) * LANES:(pair0 + p + 1) * LANES]
                 for p in range(PAIRS_PER_KV)], axis=0)
            k_win = jnp.concatenate([k0_ref[window, cols], k1_ref[window, cols]], axis=0)
            return lax.dot_general(k_win, q_sub, (((1,), (1,)), ((), ())),
                                   preferred_element_type=F32)

        def finish(n, g, sub, s_t):
            pair0 = g * PAIRS_PER_KV
            band = slice(n * ATTN_BLOCK, (n + 2) * ATTN_BLOCK)
            vt = vt_ref[g * HEAD_DIM:(g + 1) * HEAD_DIM, band]
            queries = pl.ds(n * ATTN_BLOCK + sub * QUERY_SUB, QUERY_SUB)
            valid = in_window
            if n == 0:
                valid = valid & ((i > 0) | (key >= ATTN_BLOCK - sub * QUERY_SUB))
            probs = []
            for e in range(HEADS_PER_VREG):
                s = s_t[e * KEY_WIN:(e + 1) * KEY_WIN]
                sink = jnp.full((1, 2 * ATTN_BLOCK), NEG_INF, F32)
                for p in range(PAIRS_PER_KV):
                    head = (pair0 + p) * HEADS_PER_VREG + e
                    sink = jnp.where(lane // QUERY_SUB == p, sinks_ref[head] * LOG2_E, sink)
                top = jnp.where(valid[:SUBLANES], s[:SUBLANES],
                                jnp.where(key_top == 0, sink, NEG_INF))
                rest = jnp.where(valid[SUBLANES:], s[SUBLANES:], NEG_INF)
                s = jnp.concatenate([top, rest], axis=0)
                m = jnp.max(s, axis=0, keepdims=True)
                p_t = jnp.exp2(s - m).astype(BF16)
                probs += [p_t, zero_keys] if sub == 0 else [zero_keys, p_t]
            p_all = jnp.concatenate(probs, axis=0)
            keep = jnp.where(lane == sub * QUERY_SUB, 0.0, 1.0).astype(BF16)
            vt_keep = vt * keep
            v_aug = jnp.concatenate(
                [jnp.concatenate([vt_keep, zero_half], axis=1),
                 jnp.concatenate([zero_half, vt_keep], axis=1),
                 ones_rows], axis=0)
            o_t = _dot(v_aug, p_all)
            inv0 = 1.0 / o_t[2 * HEAD_DIM:2 * HEAD_DIM + 1]
            inv1 = 1.0 / o_t[2 * HEAD_DIM + SUBLANES:2 * HEAD_DIM + SUBLANES + 1]
            out_t = jnp.concatenate([o_t[:HEAD_DIM] * inv0,
                                     o_t[HEAD_DIM:2 * HEAD_DIM] * inv1], axis=0)
            out = jnp.transpose(out_t)
            for p in range(PAIRS_PER_KV):
                pair_cols = slice((pair0 + p) * LANES, (pair0 + p + 1) * LANES)
                o_ref[queries, pair_cols] = (
                    out[p * QUERY_SUB:(p + 1) * QUERY_SUB] * gate_ref[queries, pair_cols]
                ).astype(BF16)

        @pl.when(i >= 0)
        def _():
            work = [(n, g, sub) for n in range(tq // ATTN_BLOCK) for g in range(N_KV_HEADS)
                    for sub in range(ATTN_BLOCK // QUERY_SUB)]
            pending = []
            for step in range(len(work) + SCORE_LOOKAHEAD):
                if step < len(work):
                    pending.append((work[step], scores(*work[step])))
                if step >= SCORE_LOOKAHEAD:
                    item, s_t = pending.pop(0)
                    finish(*item, s_t)

        finish_previous_tile()
        r_ref[...] = x1 + _dot(o_ref[...], wout_ref[...])

    @pl.when(t < n_tiles)
    def _():
        attend_this_tile()

    @pl.when(t == n_tiles)
    def _():
        finish_previous_tile()


def _resident(shape):
    return pl.BlockSpec(shape, lambda *_: (0,) * len(shape), pipeline_mode=pl.Buffered(1))


def _rope_tables(tile, tiles_per_seq):
    lane = jnp.arange(LANES)
    inv_freq = ROPE_THETA ** (-(2 * (lane % HALF_DIM)).astype(F32) / HEAD_DIM)
    inside = jnp.arange(tile, dtype=F32)[:, None] * inv_freq[None, :]
    base = (jnp.arange(tiles_per_seq, dtype=F32) * tile)[:, None] * inv_freq[None, :]
    return jnp.cos(inside), jnp.sin(inside), jnp.cos(base), jnp.sin(base)


def _tile_maps(n_tiles, tiles_per_seq):
    def split(tile):
        return tile // tiles_per_seq, tile % tiles_per_seq

    def this_tile(t):
        return split(jnp.minimum(t, n_tiles - 1))

    def previous_tile(t):
        return split(jnp.maximum(t - 1, 0))

    return this_tile, previous_tile


def _layer_a(x, w_in, w_group, scale, w_out, ln_g, ln_b, w_k, w_v, rope_tables,
             next_w_qg, next_w_out):
    batch, seq, _ = x.shape
    tm = TILE_A
    tiles_per_seq = seq // tm
    n_tiles = batch * tiles_per_seq
    this_tile, previous_tile = _tile_maps(n_tiles, tiles_per_seq)
    lagged = lambda width: pl.BlockSpec((1, tm, width), lambda t: (*previous_tile(t), 0))
    in_hbm = pl.BlockSpec(memory_space=pl.ANY)
    cast_rows = D_MODEL // n_tiles
    assert cast_rows * n_tiles == D_MODEL and cast_rows % BF16_SUBLANES == 0
    cast_slice = lambda width: pl.BlockSpec(
        (cast_rows, width), lambda t: (jnp.minimum(t, n_tiles - 1), 0))
    return pl.pallas_call(
        functools.partial(_layer_a_kernel, n_tiles=n_tiles, tiles_per_seq=tiles_per_seq),
        grid=(n_tiles + 1,),
        in_specs=[
            pl.BlockSpec((1, tm, D_MODEL), lambda t: (*this_tile(t), 0)),
            in_hbm, in_hbm,
            _resident((1, D_MODEL)),
            in_hbm,
            _resident((1, D_MODEL)),
            _resident((1, D_MODEL)),
            in_hbm, in_hbm,
            _resident((tm, LANES)), _resident((tm, LANES)),
            _resident((tiles_per_seq, LANES)), _resident((tiles_per_seq, LANES)),
            cast_slice(2 * D_MODEL), cast_slice(D_MODEL),
        ],
        out_specs=[lagged(D_MODEL), lagged(KV_WIDTH), lagged(KV_WIDTH),
                   cast_slice(2 * D_MODEL), cast_slice(D_MODEL)],
        out_shape=[
            jax.ShapeDtypeStruct((batch, seq, D_MODEL), F32),
            jax.ShapeDtypeStruct((batch, seq, KV_WIDTH), BF16),
            jax.ShapeDtypeStruct((batch, seq, KV_WIDTH), BF16),
            jax.ShapeDtypeStruct((D_MODEL, 2 * D_MODEL), BF16),
            jax.ShapeDtypeStruct((D_MODEL, D_MODEL), BF16),
        ],
        scratch_shapes=[
            pltpu.VMEM((D_MODEL, D_MODEL), BF16),
            pltpu.VMEM((D_MODEL, D_MODEL), BF16),
            pltpu.VMEM((N_POOL_GROUPS, POOL_GROUP_DIM, POOL_GROUP_DIM), BF16),
            pltpu.VMEM((D_MODEL, D_MODEL), BF16),
            pltpu.VMEM((D_MODEL, 2 * KV_WIDTH), BF16),
            pltpu.VMEM((2, CAST_ROWS, CAST_COLS), F32),
            pltpu.SemaphoreType.DMA((2,)),
            pltpu.VMEM((N_POOL_GROUPS, tm, D_MODEL), BF16),
            pltpu.VMEM((tm, D_MODEL), F32),
            pltpu.VMEM((MAX_WINDOW, D_MODEL), F32),
            pltpu.VMEM((tm, D_MODEL), BF16),
            pltpu.VMEM((tm, D_MODEL), F32),
        ],
        compiler_params=pltpu.CompilerParams(
            dimension_semantics=("arbitrary",),
            vmem_limit_bytes=VMEM_LIMIT_BYTES),
        name="yoco_pool_layer",
    )(x, w_in, w_group, scale, w_out, ln_g, ln_b, w_k, w_v, *rope_tables, next_w_qg, next_w_out)


def _layer_b(x1, k, v, w_qg, sinks, w_out, ln_g, ln_b, rope_tables):
    batch, seq, _ = x1.shape
    tq = TILE_B
    blocks_per_tile = tq // ATTN_BLOCK
    tiles_per_seq = seq // tq
    n_tiles = batch * tiles_per_seq
    this_tile, previous_tile = _tile_maps(n_tiles, tiles_per_seq)
    tok = lambda width: pl.BlockSpec((1, tq, width), lambda t: (*this_tile(t), 0))

    def previous_block(t):
        b, i = this_tile(t)
        return b, jnp.maximum(i * blocks_per_tile - 1, 0), 0

    prev = pl.BlockSpec((1, ATTN_BLOCK, KV_WIDTH), previous_block)
    padded = N_KV_HEADS * LANES
    return pl.pallas_call(
        functools.partial(_layer_b_kernel, n_tiles=n_tiles, tiles_per_seq=tiles_per_seq),
        grid=(n_tiles + 1,),
        in_specs=[
            pl.BlockSpec(memory_space=pltpu.SMEM),
            tok(D_MODEL),
            tok(KV_WIDTH), prev, tok(KV_WIDTH), prev,
            _resident((D_MODEL, 2 * D_MODEL)),
            _resident((D_MODEL, D_MODEL)),
            _resident((1, D_MODEL)),
            _resident((1, D_MODEL)),
            _resident((tq, LANES)), _resident((tq, LANES)),
            _resident((tiles_per_seq, LANES)), _resident((tiles_per_seq, LANES)),
        ],
        out_specs=pl.BlockSpec((1, tq, D_MODEL), lambda t: (*previous_tile(t), 0)),
        out_shape=jax.ShapeDtypeStruct((batch, seq, D_MODEL), F32),
        scratch_shapes=[
            pltpu.VMEM((tq, D_MODEL), BF16),
            pltpu.VMEM((tq + ATTN_BLOCK, padded), BF16),
            pltpu.VMEM((tq + ATTN_BLOCK, padded), BF16),
            pltpu.VMEM((KV_WIDTH, tq + ATTN_BLOCK), BF16),
            pltpu.VMEM((tq, D_MODEL), BF16),
            pltpu.VMEM((tq, D_MODEL), F32),
            pltpu.VMEM((tq, D_MODEL), F32),
        ],
        compiler_params=pltpu.CompilerParams(
            dimension_semantics=("arbitrary",),
            vmem_limit_bytes=VMEM_LIMIT_BYTES),
        name="yoco_swa_layer",
    )(sinks, x1, k, k, v, v, w_qg, w_out, ln_g, ln_b, *rope_tables)


def kernel(x, ln_g, ln_b, a_w_in, a_w_group, a_scale, a_w_out, b_w_k, b_w_v, b_w_qg, b_sinks, b_w_out):
    assert a_w_in.shape[0] == 1 and b_w_qg.shape[0] == 1
    assert TILE_A == TILE_B
    rope_tables = _rope_tables(TILE_A, x.shape[1] // TILE_A)
    x1, k, v, w_qg_bf, b_w_out_bf = _layer_a(
        x, a_w_in[0], a_w_group[0], a_scale[0][None, :], a_w_out[0],
        ln_g[0][None, :], ln_b[0][None, :], b_w_k, b_w_v, rope_tables, b_w_qg[0], b_w_out[0])
    return _layer_b(
        x1, k, v, w_qg_bf, b_sinks[0], b_w_out_bf,
        ln_g[1][None, :], ln_b[1][None, :], rope_tables)
```

```python
import functools

import jax
import jax.numpy as jnp
from jax import lax
from jax.experimental import pallas as pl
from jax.experimental.pallas import tpu as pltpu

D_MODEL = 2048
DEPTH = 2
POOL_WINDOWS = (2, 4, 8, 16)
N_POOL_GROUPS = len(POOL_WINDOWS)
POOL_GROUP_DIM = D_MODEL // N_POOL_GROUPS
MAX_WINDOW = max(POOL_WINDOWS)
HEAD_DIM = 64
HALF_DIM = HEAD_DIM // 2
N_Q_HEADS = D_MODEL // HEAD_DIM
N_KV_HEADS = N_Q_HEADS // 8
GQA_GROUP = N_Q_HEADS // N_KV_HEADS
KV_WIDTH = N_KV_HEADS * HEAD_DIM
ATTN_BLOCK = 128
ROPE_THETA = 10000.0
LN_EPS = 1e-5
NEG_INF = -1e30
LOG2_E = 1.4426950408889634
DEEPNORM_ALPHA = (2 * DEPTH) ** 0.25

LANES = 128
SUBLANES = 8
BF16_SUBLANES = 16
QUERY_SUB = 64
KEY_WIN = QUERY_SUB + ATTN_BLOCK
SCORE_LOOKAHEAD = 3
HEADS_PER_VREG = LANES // HEAD_DIM
PAIRS_PER_KV = GQA_GROUP // HEADS_PER_VREG
VMEM_LIMIT_BYTES = 56 * 1024 * 1024
TILE_A = 256
TILE_B = 256
CAST_ROWS = 1024
CAST_COLS = 512
CAST_SLOTS = 4

F32 = jnp.float32
BF16 = jnp.bfloat16


def _dot(a, b):
    return jnp.dot(a, b, preferred_element_type=F32)


def _silu(z):
    half = 0.5 * z
    return half + half * jnp.tanh(half)


def _layer_norm(r, g, b):
    mu = jnp.mean(r, axis=-1, keepdims=True)
    c = r - mu
    var = jnp.mean(c * c, axis=-1, keepdims=True)
    return c * lax.rsqrt(var + LN_EPS / DEEPNORM_ALPHA ** 2) * g + b


def _tile_rope_tables(i, cos_in_ref, sin_in_ref, cos_base_ref, sin_base_ref):
    cb, sb = cos_base_ref[pl.ds(i, 1), :], sin_base_ref[pl.ds(i, 1), :]
    ci, si = cos_in_ref[...], sin_in_ref[...]
    lane = lax.broadcasted_iota(jnp.int32, (1, LANES), 1)
    sign = jnp.where((lane % HEAD_DIM) < HALF_DIM, -1.0, 1.0)
    return cb * ci - sb * si, (sb * ci + cb * si) * sign


def _rope(t, cos, sin_signed):
    width = t.shape[1]
    reps = width // LANES
    cos_w = jnp.concatenate([cos] * reps, axis=1)
    sin_w = jnp.concatenate([sin_signed] * reps, axis=1)
    lane = lax.broadcasted_iota(jnp.int32, (1, width), 1)
    first_half = (lane % HEAD_DIM) < HALF_DIM
    partner = jnp.where(first_half,
                        pltpu.roll(t, width - HALF_DIM, axis=1),
                        pltpu.roll(t, HALF_DIM, axis=1))
    return t * cos_w + partner * sin_w


def _cast_jobs(win_hbm, wg_hbm, wout_hbm, wk_hbm, wv_hbm, scale_ref,
               wmix_ref, wz_ref, wg_ref, wout_ref, wkv_ref):
    def fold(g):
        cols = slice(g * POOL_GROUP_DIM, (g + 1) * POOL_GROUP_DIM)
        return lambda chunk: (_dot(chunk, wg_ref[g]) * scale_ref[:, cols]).astype(BF16)

    jobs = [(wg_hbm.at[g], wg_ref.at[g], None) for g in range(N_POOL_GROUPS)]
    for r in range(0, D_MODEL, CAST_ROWS):
        for g in range(N_POOL_GROUPS):
            rows, cols = pl.ds(r, CAST_ROWS), pl.ds(g * POOL_GROUP_DIM, POOL_GROUP_DIM)
            jobs.append((win_hbm.at[rows, cols], wmix_ref.at[rows, cols], fold(g)))
    for src, col0, dst, scale in ((win_hbm, D_MODEL, wz_ref, None),
                                  (wout_hbm, 0, wout_ref, 1.0 / DEEPNORM_ALPHA)):
        for r in range(0, D_MODEL, CAST_ROWS):
            for c in range(0, D_MODEL, CAST_COLS):
                rows = pl.ds(r, CAST_ROWS)
                jobs.append((src.at[rows, pl.ds(col0 + c, CAST_COLS)],
                             dst.at[rows, pl.ds(c, CAST_COLS)], scale))
    for j, src in enumerate((wk_hbm, wv_hbm)):
        for r in range(0, D_MODEL, CAST_ROWS):
            jobs.append((src.at[pl.ds(r, CAST_ROWS), :],
                         wkv_ref.at[pl.ds(r, CAST_ROWS), pl.ds(j * KV_WIDTH, KV_WIDTH)], None))
    return jobs


def _load_and_cast_weights(jobs, stage_ref, sem_ref):
    slots = stage_ref.shape[0]

    def copy(k):
        src, dst, _ = jobs[k]
        rows, cols = dst.shape
        slot = k % slots
        return pltpu.make_async_copy(
            src, stage_ref.at[slot, pl.ds(0, rows), pl.ds(0, cols)], sem_ref.at[slot])

    for k in range(min(slots - 1, len(jobs))):
        copy(k).start()
    for k, (_, dst, transform) in enumerate(jobs):
        if k + slots - 1 < len(jobs):
            copy(k + slots - 1).start()
        copy(k).wait()
        rows, cols = dst.shape
        chunk = stage_ref[k % slots, pl.ds(0, rows), pl.ds(0, cols)]
        if isinstance(transform, float):
            dst[...] = (chunk * transform).astype(BF16)
        elif transform is None:
            dst[...] = chunk.astype(BF16)
        else:
            dst[...] = transform(chunk.astype(BF16))


def _layer_a_kernel(x_ref, win_hbm, wg_hbm, scale_ref, wout_hbm, g_ref, b_ref, wk_hbm, wv_hbm,
                    cos_in_ref, sin_in_ref, cos_base_ref, sin_base_ref,
                    next_wqg_ref, next_wout_ref,
                    x1_ref, k_ref, v_ref, next_wqg_bf_ref, next_wout_bf_ref,
                    wmix_ref, wz_ref, wg_ref, wout_ref, wkv_ref, stage_ref, sem_ref,
                    pooled_ref, gate_ref, carry_ref, y_ref, r_ref, *, n_tiles, tiles_per_seq):
    tm = x_ref.shape[1]
    t = pl.program_id(0)

    @pl.when(t == 0)
    def _():
        _load_and_cast_weights(
            _cast_jobs(win_hbm, wg_hbm, wout_hbm, wk_hbm, wv_hbm, scale_ref,
                       wmix_ref, wz_ref, wg_ref, wout_ref, wkv_ref), stage_ref, sem_ref)
        r_ref[...] = jnp.zeros_like(r_ref)
        carry_ref[...] = jnp.zeros_like(carry_ref)

    def normalise_previous_tile():
        next_wqg_bf_ref[...] = next_wqg_ref[...].astype(BF16)
        next_wout_bf_ref[...] = (next_wout_ref[...] * (1.0 / DEEPNORM_ALPHA)).astype(BF16)
        x1 = _layer_norm(r_ref[...], g_ref[...], b_ref[...])
        x1_ref[0] = x1
        return x1.astype(BF16)

    def project_kv_previous_tile(x1_bf):
        kv = _dot(x1_bf, wkv_ref[...])
        i_prev = jnp.maximum(t - 1, 0) % tiles_per_seq
        k = _rope(kv[:, :KV_WIDTH], *_tile_rope_tables(
            i_prev, cos_in_ref, sin_in_ref, cos_base_ref, sin_base_ref))
        k_ref[0] = k.astype(BF16)
        v_ref[0] = kv[:, KV_WIDTH:].astype(BF16)

    def mix_this_tile():
        i = t % tiles_per_seq
        x = x_ref[0]
        xb = x.astype(BF16)
        pos1 = lax.broadcasted_iota(jnp.int32, (tm, 1), 0) + (i * tm + 1)
        gate_ref[...] = _silu(_dot(xb, wz_ref[...]))
        for c in range(0, D_MODEL, POOL_GROUP_DIM):
            cols = slice(c, c + POOL_GROUP_DIM)
            history = jnp.where(i > 0, carry_ref[:, cols], 0.0)
            acc = jnp.concatenate([history, x[:, cols]], axis=0)
            shift = 1
            for g, w in enumerate(POOL_WINDOWS):
                while shift < w:
                    acc = acc + pltpu.roll(acc, shift, axis=0)
                    shift *= 2
                inv_count = 1.0 / jnp.minimum(pos1, w).astype(F32)
                pooled_ref[g, :, cols] = (acc[MAX_WINDOW:, :] * inv_count - x[:, cols]).astype(BF16)
        carry_ref[...] = x[tm - MAX_WINDOW:, :]
        for g in range(N_POOL_GROUPS):
            lo, hi = g * POOL_GROUP_DIM, (g + 1) * POOL_GROUP_DIM
            mixed = _dot(pooled_ref[g], wmix_ref[:, lo:hi])
            y_ref[:, lo:hi] = (mixed * gate_ref[:, lo:hi]).astype(BF16)
        r_ref[...] = x + _dot(y_ref[...], wout_ref[...])

    @pl.when(t < n_tiles)
    def _():
        x1_bf = normalise_previous_tile()
        mix_this_tile()
        project_kv_previous_tile(x1_bf)

    @pl.when(t == n_tiles)
    def _():
        project_kv_previous_tile(normalise_previous_tile())


def _pad_heads(t):
    lane = lax.broadcasted_iota(jnp.int32, (1, LANES), 1)
    low = lane < HEAD_DIM
    lo_cols, hi_cols = [], []
    for j in range(KV_WIDTH // LANES):
        col = t[:, j * LANES:(j + 1) * LANES]
        swapped = pltpu.roll(col, HEAD_DIM, axis=1)
        lo_cols += [jnp.where(low, col, 0.0), jnp.where(low, swapped, 0.0)]
        hi_cols += [jnp.where(low, 0.0, swapped), jnp.where(low, 0.0, col)]
    return (jnp.concatenate(lo_cols, axis=1).astype(BF16),
            jnp.concatenate(hi_cols, axis=1).astype(BF16))


def _layer_b_kernel(sinks_ref, x1_ref, kc_ref, kp_ref, vc_ref, vp_ref, wqg_ref, wout_ref,
                    g_ref, b_ref, cos_in_ref, sin_in_ref, cos_base_ref, sin_base_ref, out_ref,
                    q_ref, k0_ref, k1_ref, vt_ref, o_ref, gate_ref, r_ref,
                    *, n_tiles, tiles_per_seq):
    tq = x1_ref.shape[1]
    t = pl.program_id(0)

    @pl.when(t == 0)
    def _():
        r_ref[...] = jnp.zeros_like(r_ref)

    def finish_previous_tile(rows=slice(None)):
        out_ref[0, rows, :] = _layer_norm(r_ref[rows, :], g_ref[...], b_ref[...])

    def attend_this_tile():
        i = t % tiles_per_seq
        x1 = x1_ref[0]
        xb = x1.astype(BF16)

        cos, sin_signed = _tile_rope_tables(i, cos_in_ref, sin_in_ref, cos_base_ref, sin_base_ref)
        q_scale = HEAD_DIM ** -0.5 * LOG2_E
        q = _rope(_dot(xb, wqg_ref[:, :D_MODEL]), cos * q_scale, sin_signed * q_scale)
        q_ref[...] = q.astype(BF16)

        kf = jnp.concatenate([kp_ref[0], kc_ref[0]], axis=0).astype(F32)
        vf = jnp.concatenate([vp_ref[0], vc_ref[0]], axis=0).astype(F32)
        k0_ref[...], k1_ref[...] = _pad_heads(kf)
        vt_ref[...] = jnp.transpose(vf).astype(BF16)

        gate_ref[...] = _silu(_dot(xb, wqg_ref[:, D_MODEL:]))

        key = lax.broadcasted_iota(jnp.int32, (KEY_WIN, 2 * ATTN_BLOCK), 0)
        qry = lax.broadcasted_iota(jnp.int32, (KEY_WIN, 2 * ATTN_BLOCK), 1) % QUERY_SUB
        in_window = (key > qry) & (key <= qry + ATTN_BLOCK)
        lane = lax.broadcasted_iota(jnp.int32, (1, 2 * ATTN_BLOCK), 1)
        key_top = lax.broadcasted_iota(jnp.int32, (SUBLANES, 2 * ATTN_BLOCK), 0)
        zero_keys = jnp.zeros((QUERY_SUB, 2 * ATTN_BLOCK), BF16)
        zero_half = jnp.zeros((HEAD_DIM, 2 * ATTN_BLOCK), BF16)
        slot_row = lax.broadcasted_iota(jnp.int32, (BF16_SUBLANES, 4 * ATTN_BLOCK), 0) // SUBLANES
        slot_col = lax.broadcasted_iota(jnp.int32, (BF16_SUBLANES, 4 * ATTN_BLOCK), 1) // (
            2 * ATTN_BLOCK)
        ones_rows = jnp.where(slot_row == slot_col, 1.0, 0.0).astype(BF16)

        def scores(n, g, sub):
            cols = slice(g * LANES, (g + 1) * LANES)
            pair0 = g * PAIRS_PER_KV
            first = n * ATTN_BLOCK + sub * QUERY_SUB
            queries = pl.ds(first, QUERY_SUB)
            window = pl.ds(first, KEY_WIN)
            q_sub = jnp.concatenate(
                [q_ref[queries, (pair0 + p) * LANES:(pair0 + p + 1) * LANES]
                 for p in range(PAIRS_PER_KV)], axis=0)
            k_win = jnp.concatenate([k0_ref[window, cols], k1_ref[window, cols]], axis=0)
            return lax.dot_general(k_win, q_sub, (((1,), (1,)), ((), ())),
                                   preferred_element_type=F32)

        def finish(n, g, sub, s_t):
            pair0 = g * PAIRS_PER_KV
            band = slice(n * ATTN_BLOCK, (n + 2) * ATTN_BLOCK)
            vt = vt_ref[g * HEAD_DIM:(g + 1) * HEAD_DIM, band]
            queries = pl.ds(n * ATTN_BLOCK + sub * QUERY_SUB, QUERY_SUB)
            valid = in_window
            if n == 0:
                valid = valid & ((i > 0) | (key >= ATTN_BLOCK - sub * QUERY_SUB))
            probs = []
            for e in range(HEADS_PER_VREG):
                s = s_t[e * KEY_WIN:(e + 1) * KEY_WIN]
                sink = jnp.full((1, 2 * ATTN_BLOCK), NEG_INF, F32)
                for p in range(PAIRS_PER_KV):
                    head = (pair0 + p) * HEADS_PER_VREG + e
                    sink = jnp.where(lane // QUERY_SUB == p, sinks_ref[head] * LOG2_E, sink)
                top = jnp.where(valid[:SUBLANES], s[:SUBLANES],
                                jnp.where(key_top == 0, sink, NEG_INF))
                rest = jnp.where(valid[SUBLANES:], s[SUBLANES:], NEG_INF)
                s = jnp.concatenate([top, rest], axis=0)
                m = jnp.max(s, axis=0, keepdims=True)
                p_t = jnp.exp2(s - m).astype(BF16)
                probs += [p_t, zero_keys] if sub == 0 else [zero_keys, p_t]
            p_all = jnp.concatenate(probs, axis=0)
            keep = jnp.where(lane == sub * QUERY_SUB, 0.0, 1.0).astype(BF16)
            vt_keep = vt * keep
            v_aug = jnp.concatenate(
                [jnp.concatenate([vt_keep, zero_half], axis=1),
                 jnp.concatenate([zero_half, vt_keep], axis=1),
                 ones_rows], axis=0)
            o_t = _dot(v_aug, p_all)
            inv0 = 1.0 / o_t[2 * HEAD_DIM:2 * HEAD_DIM + 1]
            inv1 = 1.0 / o_t[2 * HEAD_DIM + SUBLANES:2 * HEAD_DIM + SUBLANES + 1]
            out_t = jnp.concatenate([o_t[:HEAD_DIM] * inv0,
                                     o_t[HEAD_DIM:2 * HEAD_DIM] * inv1], axis=0)
            out = jnp.transpose(out_t)
            for p in range(PAIRS_PER_KV):
                pair_cols = slice((pair0 + p) * LANES, (pair0 + p + 1) * LANES)
                o_ref[queries, pair_cols] = (
                    out[p * QUERY_SUB:(p + 1) * QUERY_SUB] * gate_ref[queries, pair_cols]
                ).astype(BF16)

        @pl.when(i >= 0)
        def _():
            work = [(n, g, sub) for n in range(tq // ATTN_BLOCK) for g in range(N_KV_HEADS)
                    for sub in range(ATTN_BLOCK // QUERY_SUB)]
            pending = []
            for step in range(len(work) + SCORE_LOOKAHEAD):
                if step < len(work):
                    pending.append((work[step], scores(*work[step])))
                if step >= SCORE_LOOKAHEAD:
                    item, s_t = pending.pop(0)
                    finish(*item, s_t)

        finish_previous_tile()
        r_ref[...] = x1 + _dot(o_ref[...], wout_ref[...])

    @pl.when(t < n_tiles)
    def _():
        attend_this_tile()

    @pl.when(t == n_tiles)
    def _():
        finish_previous_tile()


def _resident(shape):
    return pl.BlockSpec(shape, lambda *_: (0,) * len(shape), pipeline_mode=pl.Buffered(1))


def _rope_tables(tile, tiles_per_seq):
    lane = jnp.arange(LANES)
    inv_freq = ROPE_THETA ** (-(2 * (lane % HALF_DIM)).astype(F32) / HEAD_DIM)
    inside = jnp.arange(tile, dtype=F32)[:, None] * inv_freq[None, :]
    base = (jnp.arange(tiles_per_seq, dtype=F32) * tile)[:, None] * inv_freq[None, :]
    return jnp.cos(inside), jnp.sin(inside), jnp.cos(base), jnp.sin(base)


def _tile_maps(n_tiles, tiles_per_seq):
    def split(tile):
        return tile // tiles_per_seq, tile % tiles_per_seq

    def this_tile(t):
        return split(jnp.minimum(t, n_tiles - 1))

    def previous_tile(t):
        return split(jnp.maximum(t - 1, 0))

    return this_tile, previous_tile


def _layer_a(x, w_in, w_group, scale, w_out, ln_g, ln_b, w_k, w_v, rope_tables,
             next_w_qg, next_w_out):
    batch, seq, _ = x.shape
    tm = TILE_A
    tiles_per_seq = seq // tm
    n_tiles = batch * tiles_per_seq
    this_tile, previous_tile = _tile_maps(n_tiles, tiles_per_seq)
    lagged = lambda width: pl.BlockSpec((1, tm, width), lambda t: (*previous_tile(t), 0))
    in_hbm = pl.BlockSpec(memory_space=pl.ANY)
    cast_rows = D_MODEL // n_tiles
    assert cast_rows * n_tiles == D_MODEL and cast_rows % BF16_SUBLANES == 0
    cast_slice = lambda width: pl.BlockSpec(
        (cast_rows, width), lambda t: (jnp.minimum(t, n_tiles - 1), 0))
    return pl.pallas_call(
        functools.partial(_layer_a_kernel, n_tiles=n_tiles, tiles_per_seq=tiles_per_seq),
        grid=(n_tiles + 1,),
        in_specs=[
            pl.BlockSpec((1, tm, D_MODEL), lambda t: (*this_tile(t), 0)),
            in_hbm, in_hbm,
            _resident((1, D_MODEL)),
            in_hbm,
            _resident((1, D_MODEL)),
            _resident((1, D_MODEL)),
            in_hbm, in_hbm,
            _resident((tm, LANES)), _resident((tm, LANES)),
            _resident((tiles_per_seq, LANES)), _resident((tiles_per_seq, LANES)),
            cast_slice(2 * D_MODEL), cast_slice(D_MODEL),
        ],
        out_specs=[lagged(D_MODEL), lagged(KV_WIDTH), lagged(KV_WIDTH),
                   cast_slice(2 * D_MODEL), cast_slice(D_MODEL)],
        out_shape=[
            jax.ShapeDtypeStruct((batch, seq, D_MODEL), F32),
            jax.ShapeDtypeStruct((batch, seq, KV_WIDTH), BF16),
            jax.ShapeDtypeStruct((batch, seq, KV_WIDTH), BF16),
            jax.ShapeDtypeStruct((D_MODEL, 2 * D_MODEL), BF16),
            jax.ShapeDtypeStruct((D_MODEL, D_MODEL), BF16),
        ],
        scratch_shapes=[
            pltpu.VMEM((D_MODEL, D_MODEL), BF16),
            pltpu.VMEM((D_MODEL, D_MODEL), BF16),
            pltpu.VMEM((N_POOL_GROUPS, POOL_GROUP_DIM, POOL_GROUP_DIM), BF16),
            pltpu.VMEM((D_MODEL, D_MODEL), BF16),
            pltpu.VMEM((D_MODEL, 2 * KV_WIDTH), BF16),
            pltpu.VMEM((CAST_SLOTS, CAST_ROWS, CAST_COLS), F32),
            pltpu.SemaphoreType.DMA((CAST_SLOTS,)),
            pltpu.VMEM((N_POOL_GROUPS, tm, D_MODEL), BF16),
            pltpu.VMEM((tm, D_MODEL), F32),
            pltpu.VMEM((MAX_WINDOW, D_MODEL), F32),
            pltpu.VMEM((tm, D_MODEL), BF16),
            pltpu.VMEM((tm, D_MODEL), F32),
        ],
        compiler_params=pltpu.CompilerParams(
            dimension_semantics=("arbitrary",),
            vmem_limit_bytes=VMEM_LIMIT_BYTES),
        name="yoco_pool_layer",
    )(x, w_in, w_group, scale, w_out, ln_g, ln_b, w_k, w_v, *rope_tables, next_w_qg, next_w_out)


def _layer_b(x1, k, v, w_qg, sinks, w_out, ln_g, ln_b, rope_tables):
    batch, seq, _ = x1.shape
    tq = TILE_B
    blocks_per_tile = tq // ATTN_BLOCK
    tiles_per_seq = seq // tq
    n_tiles = batch * tiles_per_seq
    this_tile, previous_tile = _tile_maps(n_tiles, tiles_per_seq)
    tok = lambda width: pl.BlockSpec((1, tq, width), lambda t: (*this_tile(t), 0))

    def previous_block(t):
        b, i = this_tile(t)
        return b, jnp.maximum(i * blocks_per_tile - 1, 0), 0

    prev = pl.BlockSpec((1, ATTN_BLOCK, KV_WIDTH), previous_block)
    padded = N_KV_HEADS * LANES
    return pl.pallas_call(
        functools.partial(_layer_b_kernel, n_tiles=n_tiles, tiles_per_seq=tiles_per_seq),
        grid=(n_tiles + 1,),
        in_specs=[
            pl.BlockSpec(memory_space=pltpu.SMEM),
            tok(D_MODEL),
            tok(KV_WIDTH), prev, tok(KV_WIDTH), prev,
            _resident((D_MODEL, 2 * D_MODEL)),
            _resident((D_MODEL, D_MODEL)),
            _resident((1, D_MODEL)),
            _resident((1, D_MODEL)),
            _resident((tq, LANES)), _resident((tq, LANES)),
            _resident((tiles_per_seq, LANES)), _resident((tiles_per_seq, LANES)),
        ],
        out_specs=pl.BlockSpec((1, tq, D_MODEL), lambda t: (*previous_tile(t), 0)),
        out_shape=jax.ShapeDtypeStruct((batch, seq, D_MODEL), F32),
        scratch_shapes=[
            pltpu.VMEM((tq, D_MODEL), BF16),
            pltpu.VMEM((tq + ATTN_BLOCK, padded), BF16),
            pltpu.VMEM((tq + ATTN_BLOCK, padded), BF16),
            pltpu.VMEM((KV_WIDTH, tq + ATTN_BLOCK), BF16),
            pltpu.VMEM((tq, D_MODEL), BF16),
            pltpu.VMEM((tq, D_MODEL), F32),
            pltpu.VMEM((tq, D_MODEL), F32),
        ],
        compiler_params=pltpu.CompilerParams(
            dimension_semantics=("arbitrary",),
            vmem_limit_bytes=VMEM_LIMIT_BYTES),
        name="yoco_swa_layer",
    )(sinks, x1, k, k, v, v, w_qg, w_out, ln_g, ln_b, *rope_tables)


def kernel(x, ln_g, ln_b, a_w_in, a_w_group, a_scale, a_w_out, b_w_k, b_w_v, b_w_qg, b_sinks, b_w_out):
    assert a_w_in.shape[0] == 1 and b_w_qg.shape[0] == 1
    assert TILE_A == TILE_B
    rope_tables = _rope_tables(TILE_A, x.shape[1] // TILE_A)
    x1, k, v, w_qg_bf, b_w_out_bf = _layer_a(
        x, a_w_in[0], a_w_group[0], a_scale[0][None, :], a_w_out[0],
        ln_g[0][None, :], ln_b[0][None, :], b_w_k, b_w_v, rope_tables, b_w_qg[0], b_w_out[0])
    return _layer_b(
        x1, k, v, w_qg_bf, b_sinks[0], b_w_out_bf,
        ln_g[1][None, :], ln_b[1][None, :], rope_tables)
```

```python
import functools

import jax
import jax.numpy as jnp
from jax import lax
from jax.experimental import pallas as pl
from jax.experimental.pallas import tpu as pltpu

D_MODEL = 2048
DEPTH = 2
POOL_WINDOWS = (2, 4, 8, 16)
N_POOL_GROUPS = len(POOL_WINDOWS)
POOL_GROUP_DIM = D_MODEL // N_POOL_GROUPS
MAX_WINDOW = max(POOL_WINDOWS)
HEAD_DIM = 64
HALF_DIM = HEAD_DIM // 2
N_Q_HEADS = D_MODEL // HEAD_DIM
N_KV_HEADS = N_Q_HEADS // 8
GQA_GROUP = N_Q_HEADS // N_KV_HEADS
KV_WIDTH = N_KV_HEADS * HEAD_DIM
ATTN_BLOCK = 128
ROPE_THETA = 10000.0
LN_EPS = 1e-5
NEG_INF = -1e30
LOG2_E = 1.4426950408889634
DEEPNORM_ALPHA = (2 * DEPTH) ** 0.25

LANES = 128
SUBLANES = 8
BF16_SUBLANES = 16
QUERY_SUB = 64
KEY_WIN = QUERY_SUB + ATTN_BLOCK
SCORE_LOOKAHEAD = 3
HEADS_PER_VREG = LANES // HEAD_DIM
PAIRS_PER_KV = GQA_GROUP // HEADS_PER_VREG
VMEM_LIMIT_BYTES = 56 * 1024 * 1024
TILE_A = 256
TILE_B = 256
CAST_ROWS = 1024
CAST_COLS = 512
CAST_SLOTS = 4

F32 = jnp.float32
BF16 = jnp.bfloat16


def _dot(a, b):
    return jnp.dot(a, b, preferred_element_type=F32)


def _silu(z):
    half = 0.5 * z
    return half + half * jnp.tanh(half)


def _layer_norm(r, g, b):
    mu = jnp.mean(r, axis=-1, keepdims=True)
    c = r - mu
    var = jnp.mean(c * c, axis=-1, keepdims=True)
    return c * lax.rsqrt(var + LN_EPS / DEEPNORM_ALPHA ** 2) * g + b


def _tile_rope_tables(i, cos_in_ref, sin_in_ref, cos_base_ref, sin_base_ref):
    cb, sb = cos_base_ref[pl.ds(i, 1), :], sin_base_ref[pl.ds(i, 1), :]
    ci, si = cos_in_ref[...], sin_in_ref[...]
    lane = lax.broadcasted_iota(jnp.int32, (1, LANES), 1)
    sign = jnp.where((lane % HEAD_DIM) < HALF_DIM, -1.0, 1.0)
    return cb * ci - sb * si, (sb * ci + cb * si) * sign


def _rope(t, cos, sin_signed):
    width = t.shape[1]
    reps = width // LANES
    cos_w = jnp.concatenate([cos] * reps, axis=1)
    sin_w = jnp.concatenate([sin_signed] * reps, axis=1)
    lane = lax.broadcasted_iota(jnp.int32, (1, width), 1)
    first_half = (lane % HEAD_DIM) < HALF_DIM
    partner = jnp.where(first_half,
                        pltpu.roll(t, width - HALF_DIM, axis=1),
                        pltpu.roll(t, HALF_DIM, axis=1))
    return t * cos_w + partner * sin_w


def _cast_jobs(win_hbm, wg_hbm, wout_hbm, wk_hbm, wv_hbm, scale_ref,
               wmix_ref, wz_ref, wg_ref, wout_ref, wkv_ref):
    def fold(g):
        cols = slice(g * POOL_GROUP_DIM, (g + 1) * POOL_GROUP_DIM)
        return lambda chunk: (_dot(chunk, wg_ref[g]) * scale_ref[:, cols]).astype(BF16)

    jobs = [(wg_hbm.at[g], wg_ref.at[g], None) for g in range(N_POOL_GROUPS)]
    for r in range(0, D_MODEL, CAST_ROWS):
        for g in range(N_POOL_GROUPS):
            rows, cols = pl.ds(r, CAST_ROWS), pl.ds(g * POOL_GROUP_DIM, POOL_GROUP_DIM)
            jobs.append((win_hbm.at[rows, cols], wmix_ref.at[rows, cols], fold(g)))
    for src, col0, dst, scale in ((win_hbm, D_MODEL, wz_ref, None),
                                  (wout_hbm, 0, wout_ref, 1.0 / DEEPNORM_ALPHA)):
        for r in range(0, D_MODEL, CAST_ROWS):
            for c in range(0, D_MODEL, CAST_COLS):
                rows = pl.ds(r, CAST_ROWS)
                jobs.append((src.at[rows, pl.ds(col0 + c, CAST_COLS)],
                             dst.at[rows, pl.ds(c, CAST_COLS)], scale))
    for j, src in enumerate((wk_hbm, wv_hbm)):
        for r in range(0, D_MODEL, CAST_ROWS):
            jobs.append((src.at[pl.ds(r, CAST_ROWS), :],
                         wkv_ref.at[pl.ds(r, CAST_ROWS), pl.ds(j * KV_WIDTH, KV_WIDTH)], None))
    return jobs


def _load_and_cast_weights(jobs, stage_ref, sem_ref):
    slots = stage_ref.shape[0]

    def copy(k):
        src, dst, _ = jobs[k]
        rows, cols = dst.shape
        slot = k % slots
        return pltpu.make_async_copy(
            src, stage_ref.at[slot, pl.ds(0, rows), pl.ds(0, cols)], sem_ref.at[slot])

    for k in range(min(slots - 1, len(jobs))):
        copy(k).start()
    for k, (_, dst, transform) in enumerate(jobs):
        if k + slots - 1 < len(jobs):
            copy(k + slots - 1).start()
        copy(k).wait()
        rows, cols = dst.shape
        chunk = stage_ref[k % slots, pl.ds(0, rows), pl.ds(0, cols)]
        if isinstance(transform, float):
            dst[...] = (chunk * transform).astype(BF16)
        elif transform is None:
            dst[...] = chunk.astype(BF16)
        else:
            dst[...] = transform(chunk.astype(BF16))


def _layer_a_kernel(x_ref, win_hbm, wg_hbm, scale_ref, wout_hbm, g_ref, b_ref, wk_hbm, wv_hbm,
                    cos_in_ref, sin_in_ref, cos_base_ref, sin_base_ref,
                    next_wqg_ref, next_wout_ref,
                    x1_ref, k_ref, v_ref, next_wqg_bf_ref, next_wout_bf_ref,
                    wmix_ref, wz_ref, wg_ref, wout_ref, wkv_ref, stage_ref, sem_ref,
                    pooled_ref, gate_ref, carry_ref, y_ref, r_ref, *, n_tiles, tiles_per_seq):
    tm = x_ref.shape[1]
    t = pl.program_id(0)

    @pl.when(t == 0)
    def _():
        _load_and_cast_weights(
            _cast_jobs(win_hbm, wg_hbm, wout_hbm, wk_hbm, wv_hbm, scale_ref,
                       wmix_ref, wz_ref, wg_ref, wout_ref, wkv_ref), stage_ref, sem_ref)
        r_ref[...] = jnp.zeros_like(r_ref)
        carry_ref[...] = jnp.zeros_like(carry_ref)

    def normalise_previous_tile():
        next_wqg_bf_ref[...] = next_wqg_ref[...].astype(BF16)
        next_wout_bf_ref[...] = (next_wout_ref[...] * (1.0 / DEEPNORM_ALPHA)).astype(BF16)
        x1 = _layer_norm(r_ref[...], g_ref[...], b_ref[...])
        x1_ref[0] = x1
        return x1.astype(BF16)

    def project_kv_previous_tile(x1_bf):
        kv = _dot(x1_bf, wkv_ref[...])
        i_prev = jnp.maximum(t - 1, 0) % tiles_per_seq
        k = _rope(kv[:, :KV_WIDTH], *_tile_rope_tables(
            i_prev, cos_in_ref, sin_in_ref, cos_base_ref, sin_base_ref))
        k_ref[0] = k.astype(BF16)
        v_ref[0] = kv[:, KV_WIDTH:].astype(BF16)

    def mix_this_tile():
        i = t % tiles_per_seq
        x = x_ref[0]
        xb = x.astype(BF16)
        pos1 = lax.broadcasted_iota(jnp.int32, (tm, 1), 0) + (i * tm + 1)
        gate_ref[...] = _silu(_dot(xb, wz_ref[...]))
        for c in range(0, D_MODEL, POOL_GROUP_DIM):
            cols = slice(c, c + POOL_GROUP_DIM)
            history = jnp.where(i > 0, carry_ref[:, cols], 0.0)
            acc = jnp.concatenate([history, x[:, cols]], axis=0)
            shift = 1
            for g, w in enumerate(POOL_WINDOWS):
                while shift < w:
                    acc = acc + pltpu.roll(acc, shift, axis=0)
                    shift *= 2
                inv_count = 1.0 / jnp.minimum(pos1, w).astype(F32)
                pooled_ref[g, :, cols] = (acc[MAX_WINDOW:, :] * inv_count - x[:, cols]).astype(BF16)
        carry_ref[...] = x[tm - MAX_WINDOW:, :]
        for g in range(N_POOL_GROUPS):
            lo, hi = g * POOL_GROUP_DIM, (g + 1) * POOL_GROUP_DIM
            mixed = _dot(pooled_ref[g], wmix_ref[:, lo:hi])
            y_ref[:, lo:hi] = (mixed * gate_ref[:, lo:hi]).astype(BF16)
        r_ref[...] = x + _dot(y_ref[...], wout_ref[...])

    @pl.when(t < n_tiles)
    def _():
        x1_bf = normalise_previous_tile()
        mix_this_tile()
        project_kv_previous_tile(x1_bf)

    @pl.when(t == n_tiles)
    def _():
        project_kv_previous_tile(normalise_previous_tile())


def _pad_heads(t):
    lane = lax.broadcasted_iota(jnp.int32, (1, LANES), 1)
    low = lane < HEAD_DIM
    lo_cols, hi_cols = [], []
    for j in range(KV_WIDTH // LANES):
        col = t[:, j * LANES:(j + 1) * LANES]
        swapped = pltpu.roll(col, HEAD_DIM, axis=1)
        lo_cols += [jnp.where(low, col, 0.0), jnp.where(low, swapped, 0.0)]
        hi_cols += [jnp.where(low, 0.0, swapped), jnp.where(low, 0.0, col)]
    return (jnp.concatenate(lo_cols, axis=1).astype(BF16),
            jnp.concatenate(hi_cols, axis=1).astype(BF16))


def _layer_b_kernel(sinks_ref, x1_ref, kc_ref, kp_ref, vc_ref, vp_ref, wqg_ref, wout_ref,
                    g_ref, b_ref, cos_in_ref, sin_in_ref, cos_base_ref, sin_base_ref, out_ref,
                    q_ref, k0_ref, k1_ref, vt_ref, o_ref, gate_ref, r_ref,
                    *, n_tiles, tiles_per_seq):
    tq = x1_ref.shape[1]
    t = pl.program_id(0)

    @pl.when(t == 0)
    def _():
        r_ref[...] = jnp.zeros_like(r_ref)

    def finish_previous_tile(rows=slice(None)):
        out_ref[0, rows, :] = _layer_norm(r_ref[rows, :], g_ref[...], b_ref[...])

    def attend_this_tile():
        i = t % tiles_per_seq
        x1 = x1_ref[0]
        xb = x1.astype(BF16)

        cos, sin_signed = _tile_rope_tables(i, cos_in_ref, sin_in_ref, cos_base_ref, sin_base_ref)
        q_scale = HEAD_DIM ** -0.5 * LOG2_E
        q = _rope(_dot(xb, wqg_ref[:, :D_MODEL]), cos * q_scale, sin_signed * q_scale)
        q_ref[...] = q.astype(BF16)

        kf = jnp.concatenate([kp_ref[0], kc_ref[0]], axis=0).astype(F32)
        vf = jnp.concatenate([vp_ref[0], vc_ref[0]], axis=0).astype(F32)
        k0_ref[...], k1_ref[...] = _pad_heads(kf)
        vt_ref[...] = jnp.transpose(vf).astype(BF16)

        gate_ref[...] = _silu(_dot(xb, wqg_ref[:, D_MODEL:]))

        key = lax.broadcasted_iota(jnp.int32, (KEY_WIN, 2 * ATTN_BLOCK), 0)
        qry = lax.broadcasted_iota(jnp.int32, (KEY_WIN, 2 * ATTN_BLOCK), 1) % QUERY_SUB
        in_window = (key > qry) & (key <= qry + ATTN_BLOCK)
        lane = lax.broadcasted_iota(jnp.int32, (1, 2 * ATTN_BLOCK), 1)
        key_top = lax.broadcasted_iota(jnp.int32, (SUBLANES, 2 * ATTN_BLOCK), 0)
        zero_keys = jnp.zeros((QUERY_SUB, 2 * ATTN_BLOCK), BF16)
        zero_half = jnp.zeros((HEAD_DIM, 2 * ATTN_BLOCK), BF16)
        slot_row = lax.broadcasted_iota(jnp.int32, (BF16_SUBLANES, 4 * ATTN_BLOCK), 0) // SUBLANES
        slot_col = lax.broadcasted_iota(jnp.int32, (BF16_SUBLANES, 4 * ATTN_BLOCK), 1) // (
            2 * ATTN_BLOCK)
        ones_rows = jnp.where(slot_row == slot_col, 1.0, 0.0).astype(BF16)

        def scores(n, g, sub):
            cols = slice(g * LANES, (g + 1) * LANES)
            pair0 = g * PAIRS_PER_KV
            first = n * ATTN_BLOCK + sub * QUERY_SUB
            queries = pl.ds(first, QUERY_SUB)
            window = pl.ds(first, KEY_WIN)
            q_sub = jnp.concatenate(
                [q_ref[queries, (pair0 + p) * LANES:(pair0 + p + 1) * LANES]
                 for p in range(PAIRS_PER_KV)], axis=0)
            k_win = jnp.concatenate([k0_ref[window, cols], k1_ref[window, cols]], axis=0)
            return lax.dot_general(k_win, q_sub, (((1,), (1,)), ((), ())),
                                   preferred_element_type=F32)

        def finish(n, g, sub, s_t):
            pair0 = g * PAIRS_PER_KV
            band = slice(n * ATTN_BLOCK, (n + 2) * ATTN_BLOCK)
            vt = vt_ref[g * HEAD_DIM:(g + 1) * HEAD_DIM, band]
            queries = pl.ds(n * ATTN_BLOCK + sub * QUERY_SUB, QUERY_SUB)
            valid = in_window
            if n == 0:
                valid = valid & ((i > 0) | (key >= ATTN_BLOCK - sub * QUERY_SUB))
            probs = []
            for e in range(HEADS_PER_VREG):
                s = s_t[e * KEY_WIN:(e + 1) * KEY_WIN]
                sink = jnp.full((1, 2 * ATTN_BLOCK), NEG_INF, F32)
                for p in range(PAIRS_PER_KV):
                    head = (pair0 + p) * HEADS_PER_VREG + e
                    sink = jnp.where(lane // QUERY_SUB == p, sinks_ref[head] * LOG2_E, sink)
                top = jnp.where(valid[:SUBLANES], s[:SUBLANES],
                                jnp.where(key_top == 0, sink, NEG_INF))
                rest = jnp.where(valid[SUBLANES:], s[SUBLANES:], NEG_INF)
                s = jnp.concatenate([top, rest], axis=0)
                m = jnp.max(s, axis=0, keepdims=True)
                p_t = jnp.exp2(s - m).astype(BF16)
                probs += [p_t, zero_keys] if sub == 0 else [zero_keys, p_t]
            p_all = jnp.concatenate(probs, axis=0)
            keep = jnp.where(lane == sub * QUERY_SUB, 0.0, 1.0).astype(BF16)
            vt_keep = vt * keep
            v_aug = jnp.concatenate(
                [jnp.concatenate([vt_keep, zero_half], axis=1),
                 jnp.concatenate([zero_half, vt_keep], axis=1),
                 ones_rows], axis=0)
            o_t = _dot(v_aug, p_all)
            inv0 = 1.0 / o_t[2 * HEAD_DIM:2 * HEAD_DIM + 1]
            inv1 = 1.0 / o_t[2 * HEAD_DIM + SUBLANES:2 * HEAD_DIM + SUBLANES + 1]
            out_t = jnp.concatenate([o_t[:HEAD_DIM] * inv0,
                                     o_t[HEAD_DIM:2 * HEAD_DIM] * inv1], axis=0)
            out = jnp.transpose(out_t)
            for p in range(PAIRS_PER_KV):
                pair_cols = slice((pair0 + p) * LANES, (pair0 + p + 1) * LANES)
                o_ref[queries, pair_cols] = (
                    out[p * QUERY_SUB:(p + 1) * QUERY_SUB] * gate_ref[queries, pair_cols]
                ).astype(BF16)

        def attend():
            work = [(n, g, sub) for n in range(tq // ATTN_BLOCK) for g in range(N_KV_HEADS)
                    for sub in range(ATTN_BLOCK // QUERY_SUB)]
            pending = []
            for step in range(len(work) + SCORE_LOOKAHEAD):
                if step < len(work):
                    pending.append((work[step], scores(*work[step])))
                if step >= SCORE_LOOKAHEAD:
                    item, s_t = pending.pop(0)
                    finish(*item, s_t)

        attend()

        finish_previous_tile()
        r_ref[...] = x1 + _dot(o_ref[...], wout_ref[...])

    @pl.when(t < n_tiles)
    def _():
        attend_this_tile()

    @pl.when(t == n_tiles)
    def _():
        finish_previous_tile()


def _resident(shape):
    return pl.BlockSpec(shape, lambda *_: (0,) * len(shape), pipeline_mode=pl.Buffered(1))


def _rope_tables(tile, tiles_per_seq):
    lane = jnp.arange(LANES)
    inv_freq = ROPE_THETA ** (-(2 * (lane % HALF_DIM)).astype(F32) / HEAD_DIM)
    inside = jnp.arange(tile, dtype=F32)[:, None] * inv_freq[None, :]
    base = (jnp.arange(tiles_per_seq, dtype=F32) * tile)[:, None] * inv_freq[None, :]
    return jnp.cos(inside), jnp.sin(inside), jnp.cos(base), jnp.sin(base)


def _tile_maps(n_tiles, tiles_per_seq):
    def split(tile):
        return tile // tiles_per_seq, tile % tiles_per_seq

    def this_tile(t):
        return split(jnp.minimum(t, n_tiles - 1))

    def previous_tile(t):
        return split(jnp.maximum(t - 1, 0))

    return this_tile, previous_tile


def _layer_a(x, w_in, w_group, scale, w_out, ln_g, ln_b, w_k, w_v, rope_tables,
             next_w_qg, next_w_out):
    batch, seq, _ = x.shape
    tm = TILE_A
    tiles_per_seq = seq // tm
    n_tiles = batch * tiles_per_seq
    this_tile, previous_tile = _tile_maps(n_tiles, tiles_per_seq)
    lagged = lambda width: pl.BlockSpec((1, tm, width), lambda t: (*previous_tile(t), 0))
    in_hbm = pl.BlockSpec(memory_space=pl.ANY)
    cast_rows = D_MODEL // n_tiles
    assert cast_rows * n_tiles == D_MODEL and cast_rows % BF16_SUBLANES == 0
    cast_slice = lambda width: pl.BlockSpec(
        (cast_rows, width), lambda t: (jnp.minimum(t, n_tiles - 1), 0))
    return pl.pallas_call(
        functools.partial(_layer_a_kernel, n_tiles=n_tiles, tiles_per_seq=tiles_per_seq),
        grid=(n_tiles + 1,),
        in_specs=[
            pl.BlockSpec((1, tm, D_MODEL), lambda t: (*this_tile(t), 0)),
            in_hbm, in_hbm,
            _resident((1, D_MODEL)),
            in_hbm,
            _resident((1, D_MODEL)),
            _resident((1, D_MODEL)),
            in_hbm, in_hbm,
            _resident((tm, LANES)), _resident((tm, LANES)),
            _resident((tiles_per_seq, LANES)), _resident((tiles_per_seq, LANES)),
            cast_slice(2 * D_MODEL), cast_slice(D_MODEL),
        ],
        out_specs=[lagged(D_MODEL), lagged(KV_WIDTH), lagged(KV_WIDTH),
                   cast_slice(2 * D_MODEL), cast_slice(D_MODEL)],
        out_shape=[
            jax.ShapeDtypeStruct((batch, seq, D_MODEL), F32),
            jax.ShapeDtypeStruct((batch, seq, KV_WIDTH), BF16),
            jax.ShapeDtypeStruct((batch, seq, KV_WIDTH), BF16),
            jax.ShapeDtypeStruct((D_MODEL, 2 * D_MODEL), BF16),
            jax.ShapeDtypeStruct((D_MODEL, D_MODEL), BF16),
        ],
        scratch_shapes=[
            pltpu.VMEM((D_MODEL, D_MODEL), BF16),
            pltpu.VMEM((D_MODEL, D_MODEL), BF16),
            pltpu.VMEM((N_POOL_GROUPS, POOL_GROUP_DIM, POOL_GROUP_DIM), BF16),
            pltpu.VMEM((D_MODEL, D_MODEL), BF16),
            pltpu.VMEM((D_MODEL, 2 * KV_WIDTH), BF16),
            pltpu.VMEM((CAST_SLOTS, CAST_ROWS, CAST_COLS), F32),
            pltpu.SemaphoreType.DMA((CAST_SLOTS,)),
            pltpu.VMEM((N_POOL_GROUPS, tm, D_MODEL), BF16),
            pltpu.VMEM((tm, D_MODEL), F32),
            pltpu.VMEM((MAX_WINDOW, D_MODEL), F32),
            pltpu.VMEM((tm, D_MODEL), BF16),
            pltpu.VMEM((tm, D_MODEL), F32),
        ],
        compiler_params=pltpu.CompilerParams(
            dimension_semantics=("arbitrary",),
            vmem_limit_bytes=VMEM_LIMIT_BYTES),
        name="yoco_pool_layer",
    )(x, w_in, w_group, scale, w_out, ln_g, ln_b, w_k, w_v, *rope_tables, next_w_qg, next_w_out)


def _layer_b(x1, k, v, w_qg, sinks, w_out, ln_g, ln_b, rope_tables):
    batch, seq, _ = x1.shape
    tq = TILE_B
    blocks_per_tile = tq // ATTN_BLOCK
    tiles_per_seq = seq // tq
    n_tiles = batch * tiles_per_seq
    this_tile, previous_tile = _tile_maps(n_tiles, tiles_per_seq)
    tok = lambda width: pl.BlockSpec((1, tq, width), lambda t: (*this_tile(t), 0))

    def previous_block(t):
        b, i = this_tile(t)
        return b, jnp.maximum(i * blocks_per_tile - 1, 0), 0

    prev = pl.BlockSpec((1, ATTN_BLOCK, KV_WIDTH), previous_block)
    padded = N_KV_HEADS * LANES
    return pl.pallas_call(
        functools.partial(_layer_b_kernel, n_tiles=n_tiles, tiles_per_seq=tiles_per_seq),
        grid=(n_tiles + 1,),
        in_specs=[
            pl.BlockSpec(memory_space=pltpu.SMEM),
            tok(D_MODEL),
            tok(KV_WIDTH), prev, tok(KV_WIDTH), prev,
            _resident((D_MODEL, 2 * D_MODEL)),
            _resident((D_MODEL, D_MODEL)),
            _resident((1, D_MODEL)),
            _resident((1, D_MODEL)),
            _resident((tq, LANES)), _resident((tq, LANES)),
            _resident((tiles_per_seq, LANES)), _resident((tiles_per_seq, LANES)),
        ],
        out_specs=pl.BlockSpec((1, tq, D_MODEL), lambda t: (*previous_tile(t), 0)),
        out_shape=jax.ShapeDtypeStruct((batch, seq, D_MODEL), F32),
        scratch_shapes=[
            pltpu.VMEM((tq, D_MODEL), BF16),
            pltpu.VMEM((tq + ATTN_BLOCK, padded), BF16),
            pltpu.VMEM((tq + ATTN_BLOCK, padded), BF16),
            pltpu.VMEM((KV_WIDTH, tq + ATTN_BLOCK), BF16),
            pltpu.VMEM((tq, D_MODEL), BF16),
            pltpu.VMEM((tq, D_MODEL), F32),
            pltpu.VMEM((tq, D_MODEL), F32),
        ],
        compiler_params=pltpu.CompilerParams(
            dimension_semantics=("arbitrary",),
            vmem_limit_bytes=VMEM_LIMIT_BYTES),
        name="yoco_swa_layer",
    )(sinks, x1, k, k, v, v, w_qg, w_out, ln_g, ln_b, *rope_tables)


def kernel(x, ln_g, ln_b, a_w_in, a_w_group, a_scale, a_w_out, b_w_k, b_w_v, b_w_qg, b_sinks, b_w_out):
    assert a_w_in.shape[0] == 1 and b_w_qg.shape[0] == 1
    assert TILE_A == TILE_B
    rope_tables = _rope_tables(TILE_A, x.shape[1] // TILE_A)
    x1, k, v, w_qg_bf, b_w_out_bf = _layer_a(
        x, a_w_in[0], a_w_group[0], a_scale[0][None, :], a_w_out[0],
        ln_g[0][None, :], ln_b[0][None, :], b_w_k, b_w_v, rope_tables, b_w_qg[0], b_w_out[0])
    return _layer_b(
        x1, k, v, w_qg_bf, b_sinks[0], b_w_out_bf,
        ln_g[1][None, :], ln_b[1][None, :], rope_tables)
```

```python
import functools

import jax
import jax.numpy as jnp
from jax import lax
from jax.experimental import pallas as pl
from jax.experimental.pallas import tpu as pltpu

D_MODEL = 2048
DEPTH = 2
POOL_WINDOWS = (2, 4, 8, 16)
N_POOL_GROUPS = len(POOL_WINDOWS)
POOL_GROUP_DIM = D_MODEL // N_POOL_GROUPS
MAX_WINDOW = max(POOL_WINDOWS)
HEAD_DIM = 64
HALF_DIM = HEAD_DIM // 2
N_Q_HEADS = D_MODEL // HEAD_DIM
N_KV_HEADS = N_Q_HEADS // 8
GQA_GROUP = N_Q_HEADS // N_KV_HEADS
KV_WIDTH = N_KV_HEADS * HEAD_DIM
ATTN_BLOCK = 128
ROPE_THETA = 10000.0
LN_EPS = 1e-5
NEG_INF = -1e30
LOG2_E = 1.4426950408889634
DEEPNORM_ALPHA = (2 * DEPTH) ** 0.25

LANES = 128
SUBLANES = 8
BF16_SUBLANES = 16
QUERY_SUB = 64
KEY_WIN = QUERY_SUB + ATTN_BLOCK
SCORE_LOOKAHEAD = 4
HEADS_PER_VREG = LANES // HEAD_DIM
PAIRS_PER_KV = GQA_GROUP // HEADS_PER_VREG
VMEM_LIMIT_BYTES = 56 * 1024 * 1024
TILE_A = 256
TILE_B = 256
CAST_ROWS = 1024
CAST_COLS = 512
CAST_SLOTS = 4

F32 = jnp.float32
BF16 = jnp.bfloat16


def _dot(a, b):
    return jnp.dot(a, b, preferred_element_type=F32)


def _silu(z):
    half = 0.5 * z
    return half + half * jnp.tanh(half)


def _layer_norm(r, g, b):
    mu = jnp.mean(r, axis=-1, keepdims=True)
    c = r - mu
    var = jnp.mean(c * c, axis=-1, keepdims=True)
    return c * lax.rsqrt(var + LN_EPS / DEEPNORM_ALPHA ** 2) * g + b


def _tile_rope_tables(i, cos_in_ref, sin_in_ref, cos_base_ref, sin_base_ref):
    cb, sb = cos_base_ref[pl.ds(i, 1), :], sin_base_ref[pl.ds(i, 1), :]
    ci, si = cos_in_ref[...], sin_in_ref[...]
    lane = lax.broadcasted_iota(jnp.int32, (1, LANES), 1)
    sign = jnp.where((lane % HEAD_DIM) < HALF_DIM, -1.0, 1.0)
    return cb * ci - sb * si, (sb * ci + cb * si) * sign


def _rope(t, cos, sin_signed):
    width = t.shape[1]
    reps = width // LANES
    cos_w = jnp.concatenate([cos] * reps, axis=1)
    sin_w = jnp.concatenate([sin_signed] * reps, axis=1)
    lane = lax.broadcasted_iota(jnp.int32, (1, width), 1)
    first_half = (lane % HEAD_DIM) < HALF_DIM
    partner = jnp.where(first_half,
                        pltpu.roll(t, width - HALF_DIM, axis=1),
                        pltpu.roll(t, HALF_DIM, axis=1))
    return t * cos_w + partner * sin_w


def _cast_jobs(win_hbm, wg_hbm, wout_hbm, wk_hbm, wv_hbm, scale_ref,
               wmix_ref, wz_ref, wg_ref, wout_ref, wkv_ref):
    def fold(g):
        cols = slice(g * POOL_GROUP_DIM, (g + 1) * POOL_GROUP_DIM)
        return lambda chunk: (_dot(chunk, wg_ref[g]) * scale_ref[:, cols]).astype(BF16)

    jobs = [(wg_hbm.at[g], wg_ref.at[g], None) for g in range(N_POOL_GROUPS)]
    for r in range(0, D_MODEL, CAST_ROWS):
        for g in range(N_POOL_GROUPS):
            rows, cols = pl.ds(r, CAST_ROWS), pl.ds(g * POOL_GROUP_DIM, POOL_GROUP_DIM)
            jobs.append((win_hbm.at[rows, cols], wmix_ref.at[rows, cols], fold(g)))
    for src, col0, dst, scale in ((win_hbm, D_MODEL, wz_ref, None),
                                  (wout_hbm, 0, wout_ref, 1.0 / DEEPNORM_ALPHA)):
        for r in range(0, D_MODEL, CAST_ROWS):
            for c in range(0, D_MODEL, CAST_COLS):
                rows = pl.ds(r, CAST_ROWS)
                jobs.append((src.at[rows, pl.ds(col0 + c, CAST_COLS)],
                             dst.at[rows, pl.ds(c, CAST_COLS)], scale))
    for j, src in enumerate((wk_hbm, wv_hbm)):
        for r in range(0, D_MODEL, CAST_ROWS):
            jobs.append((src.at[pl.ds(r, CAST_ROWS), :],
                         wkv_ref.at[pl.ds(r, CAST_ROWS), pl.ds(j * KV_WIDTH, KV_WIDTH)], None))
    return jobs


def _load_and_cast_weights(jobs, stage_ref, sem_ref):
    slots = stage_ref.shape[0]

    def copy(k):
        src, dst, _ = jobs[k]
        rows, cols = dst.shape
        slot = k % slots
        return pltpu.make_async_copy(
            src, stage_ref.at[slot, pl.ds(0, rows), pl.ds(0, cols)], sem_ref.at[slot])

    for k in range(min(slots - 1, len(jobs))):
        copy(k).start()
    for k, (_, dst, transform) in enumerate(jobs):
        if k + slots - 1 < len(jobs):
            copy(k + slots - 1).start()
        copy(k).wait()
        rows, cols = dst.shape
        chunk = stage_ref[k % slots, pl.ds(0, rows), pl.ds(0, cols)]
        if isinstance(transform, float):
            dst[...] = (chunk * transform).astype(BF16)
        elif transform is None:
            dst[...] = chunk.astype(BF16)
        else:
            dst[...] = transform(chunk.astype(BF16))


def _layer_a_kernel(x_ref, win_hbm, wg_hbm, scale_ref, wout_hbm, g_ref, b_ref, wk_hbm, wv_hbm,
                    cos_in_ref, sin_in_ref, cos_base_ref, sin_base_ref,
                    next_wqg_ref, next_wout_ref,
                    x1_ref, k_ref, v_ref, next_wqg_bf_ref, next_wout_bf_ref,
                    wmix_ref, wz_ref, wg_ref, wout_ref, wkv_ref, stage_ref, sem_ref,
                    pooled_ref, gate_ref, carry_ref, y_ref, r_ref, *, n_tiles, tiles_per_seq):
    tm = x_ref.shape[1]
    t = pl.program_id(0)

    @pl.when(t == 0)
    def _():
        _load_and_cast_weights(
            _cast_jobs(win_hbm, wg_hbm, wout_hbm, wk_hbm, wv_hbm, scale_ref,
                       wmix_ref, wz_ref, wg_ref, wout_ref, wkv_ref), stage_ref, sem_ref)
        r_ref[...] = jnp.zeros_like(r_ref)
        carry_ref[...] = jnp.zeros_like(carry_ref)

    def normalise_previous_tile():
        next_wqg_bf_ref[...] = next_wqg_ref[...].astype(BF16)
        next_wout_bf_ref[...] = (next_wout_ref[...] * (1.0 / DEEPNORM_ALPHA)).astype(BF16)
        x1 = _layer_norm(r_ref[...], g_ref[...], b_ref[...])
        x1_ref[0] = x1
        return x1.astype(BF16)

    def project_kv_previous_tile(x1_bf):
        kv = _dot(x1_bf, wkv_ref[...])
        i_prev = jnp.maximum(t - 1, 0) % tiles_per_seq
        k = _rope(kv[:, :KV_WIDTH], *_tile_rope_tables(
            i_prev, cos_in_ref, sin_in_ref, cos_base_ref, sin_base_ref))
        k_ref[0] = k.astype(BF16)
        v_ref[0] = kv[:, KV_WIDTH:].astype(BF16)

    def mix_this_tile():
        i = t % tiles_per_seq
        x = x_ref[0]
        xb = x.astype(BF16)
        pos1 = lax.broadcasted_iota(jnp.int32, (tm, 1), 0) + (i * tm + 1)
        gate_ref[...] = _silu(_dot(xb, wz_ref[...]))
        for c in range(0, D_MODEL, POOL_GROUP_DIM):
            cols = slice(c, c + POOL_GROUP_DIM)
            history = jnp.where(i > 0, carry_ref[:, cols], 0.0)
            acc = jnp.concatenate([history, x[:, cols]], axis=0)
            shift = 1
            for g, w in enumerate(POOL_WINDOWS):
                while shift < w:
                    acc = acc + pltpu.roll(acc, shift, axis=0)
                    shift *= 2
                inv_count = 1.0 / jnp.minimum(pos1, w).astype(F32)
                pooled_ref[g, :, cols] = (acc[MAX_WINDOW:, :] * inv_count - x[:, cols]).astype(BF16)
        carry_ref[...] = x[tm - MAX_WINDOW:, :]
        for g in range(N_POOL_GROUPS):
            lo, hi = g * POOL_GROUP_DIM, (g + 1) * POOL_GROUP_DIM
            mixed = _dot(pooled_ref[g], wmix_ref[:, lo:hi])
            y_ref[:, lo:hi] = (mixed * gate_ref[:, lo:hi]).astype(BF16)
        r_ref[...] = x + _dot(y_ref[...], wout_ref[...])

    @pl.when(t < n_tiles)
    def _():
        x1_bf = normalise_previous_tile()
        mix_this_tile()
        project_kv_previous_tile(x1_bf)

    @pl.when(t == n_tiles)
    def _():
        project_kv_previous_tile(normalise_previous_tile())


def _pad_heads(t):
    lane = lax.broadcasted_iota(jnp.int32, (1, LANES), 1)
    low = lane < HEAD_DIM
    lo_cols, hi_cols = [], []
    for j in range(KV_WIDTH // LANES):
        col = t[:, j * LANES:(j + 1) * LANES]
        swapped = pltpu.roll(col, HEAD_DIM, axis=1)
        lo_cols += [jnp.where(low, col, 0.0), jnp.where(low, swapped, 0.0)]
        hi_cols += [jnp.where(low, 0.0, swapped), jnp.where(low, 0.0, col)]
    return (jnp.concatenate(lo_cols, axis=1).astype(BF16),
            jnp.concatenate(hi_cols, axis=1).astype(BF16))


def _layer_b_kernel(sinks_ref, x1_ref, kc_ref, kp_ref, vc_ref, vp_ref, wqg_ref, wout_ref,
                    g_ref, b_ref, cos_in_ref, sin_in_ref, cos_base_ref, sin_base_ref, out_ref,
                    q_ref, k0_ref, k1_ref, vt_ref, o_ref, gate_ref, r_ref,
                    *, n_tiles, tiles_per_seq):
    tq = x1_ref.shape[1]
    t = pl.program_id(0)

    @pl.when(t == 0)
    def _():
        r_ref[...] = jnp.zeros_like(r_ref)

    def finish_previous_tile(rows=slice(None)):
        out_ref[0, rows, :] = _layer_norm(r_ref[rows, :], g_ref[...], b_ref[...])

    def attend_this_tile():
        i = t % tiles_per_seq
        x1 = x1_ref[0]
        xb = x1.astype(BF16)

        cos, sin_signed = _tile_rope_tables(i, cos_in_ref, sin_in_ref, cos_base_ref, sin_base_ref)
        q_scale = HEAD_DIM ** -0.5 * LOG2_E
        q = _rope(_dot(xb, wqg_ref[:, :D_MODEL]), cos * q_scale, sin_signed * q_scale)
        q_ref[...] = q.astype(BF16)

        kf = jnp.concatenate([kp_ref[0], kc_ref[0]], axis=0).astype(F32)
        vf = jnp.concatenate([vp_ref[0], vc_ref[0]], axis=0).astype(F32)
        k0_ref[...], k1_ref[...] = _pad_heads(kf)
        vt_ref[...] = jnp.transpose(vf).astype(BF16)

        gate_ref[...] = _silu(_dot(xb, wqg_ref[:, D_MODEL:]))

        key = lax.broadcasted_iota(jnp.int32, (KEY_WIN, 2 * ATTN_BLOCK), 0)
        qry = lax.broadcasted_iota(jnp.int32, (KEY_WIN, 2 * ATTN_BLOCK), 1) % QUERY_SUB
        in_window = (key > qry) & (key <= qry + ATTN_BLOCK)
        lane = lax.broadcasted_iota(jnp.int32, (1, 2 * ATTN_BLOCK), 1)
        key_top = lax.broadcasted_iota(jnp.int32, (SUBLANES, 2 * ATTN_BLOCK), 0)
        zero_keys = jnp.zeros((QUERY_SUB, 2 * ATTN_BLOCK), BF16)
        zero_half = jnp.zeros((HEAD_DIM, 2 * ATTN_BLOCK), BF16)
        slot_row = lax.broadcasted_iota(jnp.int32, (BF16_SUBLANES, 4 * ATTN_BLOCK), 0) // SUBLANES
        slot_col = lax.broadcasted_iota(jnp.int32, (BF16_SUBLANES, 4 * ATTN_BLOCK), 1) // (
            2 * ATTN_BLOCK)
        ones_rows = jnp.where(slot_row == slot_col, 1.0, 0.0).astype(BF16)

        def scores(n, g, sub):
            cols = slice(g * LANES, (g + 1) * LANES)
            pair0 = g * PAIRS_PER_KV
            first = n * ATTN_BLOCK + sub * QUERY_SUB
            queries = pl.ds(first, QUERY_SUB)
            window = pl.ds(first, KEY_WIN)
            q_sub = jnp.concatenate(
                [q_ref[queries, (pair0 + p) * LANES:(pair0 + p + 1) * LANES]
                 for p in range(PAIRS_PER_KV)], axis=0)
            k_win = jnp.concatenate([k0_ref[window, cols], k1_ref[window, cols]], axis=0)
            return lax.dot_general(k_win, q_sub, (((1,), (1,)), ((), ())),
                                   preferred_element_type=F32)

        def finish(n, g, sub, s_t):
            pair0 = g * PAIRS_PER_KV
            band = slice(n * ATTN_BLOCK, (n + 2) * ATTN_BLOCK)
            vt = vt_ref[g * HEAD_DIM:(g + 1) * HEAD_DIM, band]
            queries = pl.ds(n * ATTN_BLOCK + sub * QUERY_SUB, QUERY_SUB)
            valid = in_window
            if n == 0:
                valid = valid & ((i > 0) | (key >= ATTN_BLOCK - sub * QUERY_SUB))
            probs = []
            for e in range(HEADS_PER_VREG):
                s = s_t[e * KEY_WIN:(e + 1) * KEY_WIN]
                sink = jnp.full((1, 2 * ATTN_BLOCK), NEG_INF, F32)
                for p in range(PAIRS_PER_KV):
                    head = (pair0 + p) * HEADS_PER_VREG + e
                    sink = jnp.where(lane // QUERY_SUB == p, sinks_ref[head] * LOG2_E, sink)
                top = jnp.where(valid[:SUBLANES], s[:SUBLANES],
                                jnp.where(key_top == 0, sink, NEG_INF))
                rest = jnp.where(valid[SUBLANES:], s[SUBLANES:], NEG_INF)
                s = jnp.concatenate([top, rest], axis=0)
                m = jnp.max(s, axis=0, keepdims=True)
                p_t = jnp.exp2(s - m).astype(BF16)
                probs += [p_t, zero_keys] if sub == 0 else [zero_keys, p_t]
            p_all = jnp.concatenate(probs, axis=0)
            keep = jnp.where(lane == sub * QUERY_SUB, 0.0, 1.0).astype(BF16)
            vt_keep = vt * keep
            v_aug = jnp.concatenate(
                [jnp.concatenate([vt_keep, zero_half], axis=1),
                 jnp.concatenate([zero_half, vt_keep], axis=1),
                 ones_rows], axis=0)
            o_t = _dot(v_aug, p_all)
            inv0 = 1.0 / o_t[2 * HEAD_DIM:2 * HEAD_DIM + 1]
            inv1 = 1.0 / o_t[2 * HEAD_DIM + SUBLANES:2 * HEAD_DIM + SUBLANES + 1]
            out_t = jnp.concatenate([o_t[:HEAD_DIM] * inv0,
                                     o_t[HEAD_DIM:2 * HEAD_DIM] * inv1], axis=0)
            out = jnp.transpose(out_t)
            for p in range(PAIRS_PER_KV):
                pair_cols = slice((pair0 + p) * LANES, (pair0 + p + 1) * LANES)
                o_ref[queries, pair_cols] = (
                    out[p * QUERY_SUB:(p + 1) * QUERY_SUB] * gate_ref[queries, pair_cols]
                ).astype(BF16)

        def attend():
            work = [(n, g, sub) for n in range(tq // ATTN_BLOCK) for g in range(N_KV_HEADS)
                    for sub in range(ATTN_BLOCK // QUERY_SUB)]
            pending = []
            for step in range(len(work) + SCORE_LOOKAHEAD):
                if step < len(work):
                    pending.append((work[step], scores(*work[step])))
                if step >= SCORE_LOOKAHEAD:
                    item, s_t = pending.pop(0)
                    finish(*item, s_t)

        attend()

        finish_previous_tile()
        r_ref[...] = x1 + _dot(o_ref[...], wout_ref[...])

    @pl.when(t < n_tiles)
    def _():
        attend_this_tile()

    @pl.when(t == n_tiles)
    def _():
        finish_previous_tile()


def _resident(shape):
    return pl.BlockSpec(shape, lambda *_: (0,) * len(shape), pipeline_mode=pl.Buffered(1))


def _rope_tables(tile, tiles_per_seq):
    lane = jnp.arange(LANES)
    inv_freq = ROPE_THETA ** (-(2 * (lane % HALF_DIM)).astype(F32) / HEAD_DIM)
    inside = jnp.arange(tile, dtype=F32)[:, None] * inv_freq[None, :]
    base = (jnp.arange(tiles_per_seq, dtype=F32) * tile)[:, None] * inv_freq[None, :]
    return jnp.cos(inside), jnp.sin(inside), jnp.cos(base), jnp.sin(base)


def _tile_maps(n_tiles, tiles_per_seq):
    def split(tile):
        return tile // tiles_per_seq, tile % tiles_per_seq

    def this_tile(t):
        return split(jnp.minimum(t, n_tiles - 1))

    def previous_tile(t):
        return split(jnp.maximum(t - 1, 0))

    return this_tile, previous_tile


def _layer_a(x, w_in, w_group, scale, w_out, ln_g, ln_b, w_k, w_v, rope_tables,
             next_w_qg, next_w_out):
    batch, seq, _ = x.shape
    tm = TILE_A
    tiles_per_seq = seq // tm
    n_tiles = batch * tiles_per_seq
    this_tile, previous_tile = _tile_maps(n_tiles, tiles_per_seq)
    lagged = lambda width: pl.BlockSpec((1, tm, width), lambda t: (*previous_tile(t), 0))
    in_hbm = pl.BlockSpec(memory_space=pl.ANY)
    cast_rows = D_MODEL // n_tiles
    assert cast_rows * n_tiles == D_MODEL and cast_rows % BF16_SUBLANES == 0
    cast_slice = lambda width: pl.BlockSpec(
        (cast_rows, width), lambda t: (jnp.minimum(t, n_tiles - 1), 0))
    return pl.pallas_call(
        functools.partial(_layer_a_kernel, n_tiles=n_tiles, tiles_per_seq=tiles_per_seq),
        grid=(n_tiles + 1,),
        in_specs=[
            pl.BlockSpec((1, tm, D_MODEL), lambda t: (*this_tile(t), 0)),
            in_hbm, in_hbm,
            _resident((1, D_MODEL)),
            in_hbm,
            _resident((1, D_MODEL)),
            _resident((1, D_MODEL)),
            in_hbm, in_hbm,
            _resident((tm, LANES)), _resident((tm, LANES)),
            _resident((tiles_per_seq, LANES)), _resident((tiles_per_seq, LANES)),
            cast_slice(2 * D_MODEL), cast_slice(D_MODEL),
        ],
        out_specs=[lagged(D_MODEL), lagged(KV_WIDTH), lagged(KV_WIDTH),
                   cast_slice(2 * D_MODEL), cast_slice(D_MODEL)],
        out_shape=[
            jax.ShapeDtypeStruct((batch, seq, D_MODEL), F32),
            jax.ShapeDtypeStruct((batch, seq, KV_WIDTH), BF16),
            jax.ShapeDtypeStruct((batch, seq, KV_WIDTH), BF16),
            jax.ShapeDtypeStruct((D_MODEL, 2 * D_MODEL), BF16),
            jax.ShapeDtypeStruct((D_MODEL, D_MODEL), BF16),
        ],
        scratch_shapes=[
            pltpu.VMEM((D_MODEL, D_MODEL), BF16),
            pltpu.VMEM((D_MODEL, D_MODEL), BF16),
            pltpu.VMEM((N_POOL_GROUPS, POOL_GROUP_DIM, POOL_GROUP_DIM), BF16),
            pltpu.VMEM((D_MODEL, D_MODEL), BF16),
            pltpu.VMEM((D_MODEL, 2 * KV_WIDTH), BF16),
            pltpu.VMEM((CAST_SLOTS, CAST_ROWS, CAST_COLS), F32),
            pltpu.SemaphoreType.DMA((CAST_SLOTS,)),
            pltpu.VMEM((N_POOL_GROUPS, tm, D_MODEL), BF16),
            pltpu.VMEM((tm, D_MODEL), F32),
            pltpu.VMEM((MAX_WINDOW, D_MODEL), F32),
            pltpu.VMEM((tm, D_MODEL), BF16),
            pltpu.VMEM((tm, D_MODEL), F32),
        ],
        compiler_params=pltpu.CompilerParams(
            dimension_semantics=("arbitrary",),
            vmem_limit_bytes=VMEM_LIMIT_BYTES),
        name="yoco_pool_layer",
    )(x, w_in, w_group, scale, w_out, ln_g, ln_b, w_k, w_v, *rope_tables, next_w_qg, next_w_out)


def _layer_b(x1, k, v, w_qg, sinks, w_out, ln_g, ln_b, rope_tables):
    batch, seq, _ = x1.shape
    tq = TILE_B
    blocks_per_tile = tq // ATTN_BLOCK
    tiles_per_seq = seq // tq
    n_tiles = batch * tiles_per_seq
    this_tile, previous_tile = _tile_maps(n_tiles, tiles_per_seq)
    tok = lambda width: pl.BlockSpec((1, tq, width), lambda t: (*this_tile(t), 0))

    def previous_block(t):
        b, i = this_tile(t)
        return b, jnp.maximum(i * blocks_per_tile - 1, 0), 0

    prev = pl.BlockSpec((1, ATTN_BLOCK, KV_WIDTH), previous_block)
    padded = N_KV_HEADS * LANES
    return pl.pallas_call(
        functools.partial(_layer_b_kernel, n_tiles=n_tiles, tiles_per_seq=tiles_per_seq),
        grid=(n_tiles + 1,),
        in_specs=[
            pl.BlockSpec(memory_space=pltpu.SMEM),
            tok(D_MODEL),
            tok(KV_WIDTH), prev, tok(KV_WIDTH), prev,
            _resident((D_MODEL, 2 * D_MODEL)),
            _resident((D_MODEL, D_MODEL)),
            _resident((1, D_MODEL)),
            _resident((1, D_MODEL)),
            _resident((tq, LANES)), _resident((tq, LANES)),
            _resident((tiles_per_seq, LANES)), _resident((tiles_per_seq, LANES)),
        ],
        out_specs=pl.BlockSpec((1, tq, D_MODEL), lambda t: (*previous_tile(t), 0)),
        out_shape=jax.ShapeDtypeStruct((batch, seq, D_MODEL), F32),
        scratch_shapes=[
            pltpu.VMEM((tq, D_MODEL), BF16),
            pltpu.VMEM((tq + ATTN_BLOCK, padded), BF16),
            pltpu.VMEM((tq + ATTN_BLOCK, padded), BF16),
            pltpu.VMEM((KV_WIDTH, tq + ATTN_BLOCK), BF16),
            pltpu.VMEM((tq, D_MODEL), BF16),
            pltpu.VMEM((tq, D_MODEL), F32),
            pltpu.VMEM((tq, D_MODEL), F32),
        ],
        compiler_params=pltpu.CompilerParams(
            dimension_semantics=("arbitrary",),
            vmem_limit_bytes=VMEM_LIMIT_BYTES),
        name="yoco_swa_layer",
    )(sinks, x1, k, k, v, v, w_qg, w_out, ln_g, ln_b, *rope_tables)


def kernel(x, ln_g, ln_b, a_w_in, a_w_group, a_scale, a_w_out, b_w_k, b_w_v, b_w_qg, b_sinks, b_w_out):
    assert a_w_in.shape[0] == 1 and b_w_qg.shape[0] == 1
    assert TILE_A == TILE_B
    rope_tables = _rope_tables(TILE_A, x.shape[1] // TILE_A)
    x1, k, v, w_qg_bf, b_w_out_bf = _layer_a(
        x, a_w_in[0], a_w_group[0], a_scale[0][None, :], a_w_out[0],
        ln_g[0][None, :], ln_b[0][None, :], b_w_k, b_w_v, rope_tables, b_w_qg[0], b_w_out[0])
    return _layer_b(
        x1, k, v, w_qg_bf, b_sinks[0], b_w_out_bf,
        ln_g[1][None, :], ln_b[1][None, :], rope_tables)
```
